```python
import jax, jax.numpy as jnp
from jax import lax
import numpy as np

D_MODEL = 4096
BATCH = 4
SEQ = 2048
DEPTH = 4
DEC_BATCH = 8
DEC_SEQ = 8
PAST_LEN = 8192
PAGE_SIZE = 128

N_A = DEPTH // 2
N_B = DEPTH - N_A
N_DENSE = (DEPTH + 1) // 2
N_MOE = DEPTH // 2
CONV_W = 3
HEAD_DIM = 128
N_HEADS = D_MODEL // HEAD_DIM
N_KV = 4
GROUP = N_HEADS // N_KV
CMP_BLOCK = 32
SEL_BLOCK = 64
N_SEL = 16
WINDOW = 512
Q_BLOCK = 32
N_KV_PARTS = 4
D_FF = 11008
N_EXPERTS = 8
TOP_K = 2
D_FF_EXPERT = 7168
D_PLE = 256
ROPE_THETA = 10000.0
LN_EPS = 1e-5
ALPHA = (2 * DEPTH) ** 0.25
BETA = (8 * DEPTH) ** -0.25
NEG = -1e30
FORCE = 1e4

kernel_name = "yoco_shortconv_nsa_decoder_step"


def layer_norm(x, g, b):
    xf = x.astype(jnp.float32)
    mu = xf.mean(-1, keepdims=True)
    var = jnp.square(xf - mu).mean(-1, keepdims=True)
    y = (xf - mu) * lax.rsqrt(var + LN_EPS) * g.astype(jnp.float32) + b.astype(jnp.float32)
    return y.astype(x.dtype)


def rope(x, pos):
    half = HEAD_DIM // 2
    inv = ROPE_THETA ** (-jnp.arange(half, dtype=jnp.float32) / half)
    ang = pos.astype(jnp.float32)[:, None] * inv[None, :]
    shape = (1, pos.shape[0]) + (1,) * (x.ndim - 3) + (half,)
    cos, sin = jnp.cos(ang).reshape(shape), jnp.sin(ang).reshape(shape)
    xf = x.astype(jnp.float32)
    x1, x2 = xf[..., :half], xf[..., half:]
    return jnp.concatenate([x1 * cos - x2 * sin, x2 * cos + x1 * sin], -1).astype(x.dtype)


def short_conv_mixer(x, conv_prev, w_in, conv_w, w_out):
    T = x.shape[1]
    b_gate, c_gate, h = jnp.split(x @ w_in, 3, axis=-1)
    u = c_gate * h
    up = jnp.concatenate([conv_prev.astype(u.dtype), u], axis=1)
    conv = sum(up[:, j:j + T] * conv_w[j] for j in range(CONV_W))
    return (b_gate * conv) @ w_out, up[:, -(CONV_W - 1):]


def swiglu(x, w_gu, w_down):
    g, u = jnp.split(x @ w_gu, 2, axis=-1)
    return (jax.nn.silu(g) * u) @ w_down


def moe_swiglu(x, w_router, b_router, w_gu, w_down):
    logits = (x @ w_router).astype(jnp.float32) + b_router.astype(jnp.float32)
    top_v, top_i = lax.top_k(logits, TOP_K)
    top_w = jax.nn.softmax(top_v, axis=-1)
    gates = jnp.sum(jax.nn.one_hot(top_i, N_EXPERTS, dtype=jnp.float32) * top_w[..., None], axis=-2).astype(x.dtype)
    y = jnp.zeros_like(x)
    for e in range(N_EXPERTS):
        y = y + gates[..., e:e + 1] * swiglu(x, w_gu[e], w_down[e])
    return y


def compress(rows_all, cmp_pe, cmp_w):
    B, T = rows_all.shape[:2]
    nc = T // CMP_BLOCK
    blk = rows_all[:, :nc * CMP_BLOCK, :2].reshape(B, nc, CMP_BLOCK, 2, N_KV, HEAD_DIM)
    blk = blk + cmp_pe[None, None, :, :, None, :].astype(blk.dtype)
    c = jnp.einsum('bnlcgd,lcde->bncge', blk, cmp_w)
    return c[:, :, 0], c[:, :, 1]


def sel_blocks(rows_all):
    B, T = rows_all.shape[:2]
    nsb = -(-T // SEL_BLOCK)
    kv = jnp.pad(rows_all[:, :, 2:4], ((0, 0), (0, nsb * SEL_BLOCK - T), (0, 0), (0, 0), (0, 0)))
    kv = kv.reshape(B, nsb, SEL_BLOCK, 2, N_KV, HEAD_DIM).transpose(0, 3, 4, 1, 2, 5)
    return kv[:, 0], kv[:, 1]


def nsa_core(q, gates, q_pos, kc, vc, ks, vs, kw, vw, kw_pos):
    B, Tq = q.shape[:2]
    scale = HEAD_DIM ** -0.5
    t = q_pos[:, None]
    qr = rope(q, q_pos)
    nc = kc.shape[1]
    s = jnp.einsum('bqgrd,bngd->bqgrn', q, kc).astype(jnp.float32) * scale
    c_ok = (((jnp.arange(nc) + 1) * CMP_BLOCK - 1)[None, :] <= t)[None, :, None, None, :]
    p_cmp = jax.nn.softmax(jnp.where(c_ok, s, NEG), axis=-1) * c_ok
    o_cmp = jnp.einsum('bqgrn,bngd->bqgrd', p_cmp.astype(vc.dtype), vc)
    nsb = ks.shape[2]
    ratio = SEL_BLOCK // CMP_BLOCK
    imp = jnp.pad(p_cmp.sum(3), ((0, 0), (0, 0), (0, 0), (0, nsb * ratio - nc)))
    imp = imp.reshape(B, Tq, N_KV, nsb, ratio).sum(-1)
    j = jnp.arange(nsb)[None, :]
    cur = t // SEL_BLOCK
    forced = ((j == 0) | (j == cur) | (j == cur - 1)).astype(jnp.float32)
    avail = j * SEL_BLOCK <= t
    score = jnp.where(avail[None, :, None, :], imp + FORCE * forced[None, :, None, :], -1.0)
    top_s, idx = lax.top_k(score, min(N_SEL, nsb))
    bi = jnp.arange(B)[:, None, None, None]
    gi = jnp.arange(N_KV)[None, None, :, None]
    k_sel = ks[bi, gi, idx]
    v_sel = vs[bi, gi, idx]
    key_pos = idx[..., None] * SEL_BLOCK + jnp.arange(SEL_BLOCK)
    s_ok = (top_s[..., None] >= 0) & (key_pos <= q_pos[None, :, None, None, None])
    s = jnp.einsum('bqgrd,bqgnld->bqgrnl', qr, k_sel).astype(jnp.float32) * scale
    p_sel = jax.nn.softmax(jnp.where(s_ok[:, :, :, None], s, NEG), axis=(-2, -1))
    o_sel = jnp.einsum('bqgrnl,bqgnld->bqgrd', p_sel.astype(v_sel.dtype), v_sel)
    dist = t - kw_pos[None, :]
    w_ok = ((dist >= 0) & (dist < WINDOW) & (kw_pos[None, :] >= 0))[None, :, None, None, :]
    s = jnp.einsum('bqgrd,bkgd->bqgrk', qr, kw).astype(jnp.float32) * scale
    p_win = jax.nn.softmax(jnp.where(w_ok, s, NEG), axis=-1)
    o_win = jnp.einsum('bqgrk,bkgd->bqgrd', p_win.astype(vw.dtype), vw)
    g = gates.astype(o_cmp.dtype)
    return g[..., 0:1] * o_cmp + g[..., 1:2] * o_sel + g[..., 2:3] * o_win


def nsa_mixer(x, pos, w_qg, w_o, ctx, blocked):
    B, T, _ = x.shape
    hd = N_HEADS * HEAD_DIM
    qg = x @ w_qg
    q = qg[..., :hd].reshape(B, T, N_KV, GROUP, HEAD_DIM)
    gates = jax.nn.sigmoid(qg[..., hd:]).reshape(B, T, N_KV, GROUP, 3)
    keys = (ctx['kc'], ctx['vc'], ctx['ks'], ctx['vs'])
    if blocked:
        def block(i):
            s0 = i * Q_BLOCK
            sl = lambda a, n, ax: lax.dynamic_slice_in_dim(a, s0, n, ax)
            return nsa_core(sl(q, Q_BLOCK, 1), sl(gates, Q_BLOCK, 1), sl(pos, Q_BLOCK, 0), *keys,
                            sl(ctx['kw'], WINDOW + Q_BLOCK, 1), sl(ctx['vw'], WINDOW + Q_BLOCK, 1),
                            sl(ctx['kw_pos'], WINDOW + Q_BLOCK, 0))
        o = jnp.moveaxis(lax.map(block, jnp.arange(T // Q_BLOCK)), 0, 1)
    else:
        o = nsa_core(q, gates, pos, *keys, ctx['kw'], ctx['vw'], ctx['kw_pos'])
    return o.reshape(B, T, hd) @ w_o


def shared_context(h, pos, past, past_rows, win_prev, w_kv, cmp_pe, cmp_w):
    B, T, _ = h.shape
    kv = (h @ w_kv).reshape(B, T, 6, N_KV, HEAD_DIM)
    rows = jnp.stack([kv[:, :, 0], kv[:, :, 1], rope(kv[:, :, 2], pos), kv[:, :, 3]], axis=2)
    win = jnp.stack([rope(kv[:, :, 4], pos), kv[:, :, 5]], axis=2)
    if past_rows is None:
        rows_all = rows
        kwin = jnp.pad(win, ((0, 0), (WINDOW, 0), (0, 0), (0, 0), (0, 0)))
        kw_pos = jnp.arange(-WINDOW, T)
        win_state = win[:, -min(WINDOW, T):]
    else:
        rows_all = jnp.concatenate([past_rows.astype(rows.dtype), rows], axis=1)
        kwin = jnp.concatenate([win_prev.astype(win.dtype), win], axis=1)
        wb = win_prev.shape[1]
        kw_pos = past - wb + jnp.arange(wb + T)
        win_state = kwin[:, -wb:]
    kc, vc = compress(rows_all, cmp_pe, cmp_w)
    ks, vs = sel_blocks(rows_all)
    ctx = dict(kc=kc, vc=vc, ks=ks, vs=vs, kw=kwin[:, :, 0], vw=kwin[:, :, 1], kw_pos=kw_pos)
    return rows, ctx, win_state


def trunk(x, p, conv_prev, past_rows, win_prev, W):
    B, T, _ = x.shape
    blocked = past_rows is None
    past = 0 if past_rows is None else past_rows.shape[1]
    pos = past + jnp.arange(T, dtype=jnp.int32)
    conv_states = []
    ctx = None
    for i in range(DEPTH):
        if i < N_A:
            prev = jnp.zeros((B, CONV_W - 1, D_MODEL), x.dtype) if conv_prev is None else conv_prev[i]
            y, st = short_conv_mixer(x, prev, W['a_w_in'][i], W['a_conv_w'][i], W['a_w_out'][i])
            conv_states.append(st)
        else:
            y = nsa_mixer(x, pos, W['b_w_qg'][i - N_A], W['b_w_o'][i - N_A], ctx, blocked)
        x = layer_norm(ALPHA * x + y, W['ln_g'][i, 0], W['ln_b'][i, 0])
        if i % 2 == 0:
            f = swiglu(x, W['ffn_w_gu'][i // 2], W['ffn_w_down'][i // 2])
        else:
            f = moe_swiglu(x, W['moe_w_router'][i // 2], W['moe_b_router'][i // 2],
                           W['moe_w_gu'][i // 2], W['moe_w_down'][i // 2])
        x = layer_norm(ALPHA * x + f, W['ln_g'][i, 1], W['ln_b'][i, 1])
        x = x + (p[i] @ W['ple_w_proj'][i]) * jax.nn.sigmoid(x @ W['ple_w_gate'][i])
        if i == N_A - 1:
            rows, ctx, win_state = shared_context(x, pos, past, past_rows, win_prev,
                                                  W['w_kv'], W['cmp_pe'], W['cmp_w'])
    return x, rows, win_state, jnp.stack(conv_states)


def setup_inputs(seed: int = 0) -> dict:
    key = jax.random.key(seed)
    keys = iter(jax.random.split(key, 32))
    f32 = jnp.float32

    def normal(shape, scale):
        return jax.random.normal(next(keys), shape, f32) * scale

    n_pages = PAST_LEN // PAGE_SIZE
    n_pool = (DEC_BATCH * n_pages * 5) // 4
    win_len = min(WINDOW, PAST_LEN)
    hd = N_HEADS * HEAD_DIM
    page_table = jax.random.permutation(next(keys), n_pool)[:DEC_BATCH * n_pages]
    page_table = page_table.reshape(DEC_BATCH, n_pages).astype(jnp.int32)
    return {
        'x_prompt': normal((BATCH, SEQ, D_MODEL), 1.0),
        'x_sample': normal((DEC_BATCH, DEC_SEQ, D_MODEL), 1.0),
        'cache_kv': normal((n_pool, PAGE_SIZE, N_KV_PARTS, N_KV, HEAD_DIM), 1.0),
        'cache_win': normal((DEC_BATCH, win_len, 2, N_KV, HEAD_DIM), 1.0),
        'state_conv': normal((N_A, DEC_BATCH, CONV_W - 1, D_MODEL), 1.0),
        'page_table': page_table,
        'p_prompt': normal((DEPTH, BATCH, SEQ, D_PLE), 1.0),
        'p_sample': normal((DEPTH, DEC_BATCH, DEC_SEQ, D_PLE), 1.0),
        'a_w_in': normal((N_A, D_MODEL, 3 * D_MODEL), D_MODEL ** -0.5),
        'a_conv_w': normal((N_A, CONV_W, D_MODEL), CONV_W ** -0.5),
        'a_w_out': normal((N_A, D_MODEL, D_MODEL), BETA * D_MODEL ** -0.5),
        'w_kv': normal((D_MODEL, 6 * N_KV * HEAD_DIM), D_MODEL ** -0.5),
        'cmp_pe': normal((CMP_BLOCK, 2, HEAD_DIM), 0.1),
        'cmp_w': normal((CMP_BLOCK, 2, HEAD_DIM, HEAD_DIM), (CMP_BLOCK * HEAD_DIM) ** -0.5),
        'b_w_qg': normal((N_B, D_MODEL, hd + 3 * N_HEADS), D_MODEL ** -0.5),
        'b_w_o': normal((N_B, hd, D_MODEL), BETA * hd ** -0.5),
        'ffn_w_gu': normal((N_DENSE, D_MODEL, 2 * D_FF), D_MODEL ** -0.5),
        'ffn_w_down': normal((N_DENSE, D_FF, D_MODEL), BETA * D_FF ** -0.5),
        'moe_w_router': normal((N_MOE, D_MODEL, N_EXPERTS), D_MODEL ** -0.5),
        'moe_b_router': normal((N_MOE, N_EXPERTS), 0.01),
        'moe_w_gu': normal((N_MOE, N_EXPERTS, D_MODEL, 2 * D_FF_EXPERT), D_MODEL ** -0.5),
        'moe_w_down': normal((N_MOE, N_EXPERTS, D_FF_EXPERT, D_MODEL), BETA * D_FF_EXPERT ** -0.5),
        'ple_w_proj': normal((DEPTH, D_PLE, D_MODEL), D_PLE ** -0.5),
        'ple_w_gate': normal((DEPTH, D_MODEL, D_MODEL), D_MODEL ** -0.5),
        'ln_g': 1.0 + normal((DEPTH, 2, D_MODEL), 0.02),
        'ln_b': normal((DEPTH, 2, D_MODEL), 0.02),
    }


def reference(x_prompt, x_sample, cache_kv, cache_win, state_conv, page_table, p_prompt, p_sample,
              a_w_in, a_conv_w, a_w_out, w_kv, cmp_pe, cmp_w, b_w_qg, b_w_o, ffn_w_gu, ffn_w_down,
              moe_w_router, moe_b_router, moe_w_gu, moe_w_down, ple_w_proj, ple_w_gate, ln_g, ln_b):
    W = dict(a_w_in=a_w_in, a_conv_w=a_conv_w, a_w_out=a_w_out, w_kv=w_kv, cmp_pe=cmp_pe, cmp_w=cmp_w,
             b_w_qg=b_w_qg, b_w_o=b_w_o, ffn_w_gu=ffn_w_gu, ffn_w_down=ffn_w_down,
             moe_w_router=moe_w_router, moe_b_router=moe_b_router, moe_w_gu=moe_w_gu,
             moe_w_down=moe_w_down, ple_w_proj=ple_w_proj, ple_w_gate=ple_w_gate, ln_g=ln_g, ln_b=ln_b)
    y_prompt, kv_rows_prompt, win_prompt, conv_prompt = trunk(x_prompt, p_prompt, None, None, None, W)
    n_seq, n_pages = page_table.shape
    past_rows = cache_kv[page_table].reshape((n_seq, n_pages * cache_kv.shape[1]) + cache_kv.shape[2:])
    y_sample, kv_rows_sample, win_sample, conv_sample = trunk(x_sample, p_sample, state_conv, past_rows, cache_win, W)
    return (y_prompt, y_sample, kv_rows_prompt, kv_rows_sample, win_prompt, win_sample, conv_prompt, conv_sample)
```

```python
import functools

import jax
import jax.numpy as jnp
from jax import lax
from jax.experimental import pallas as pl
from jax.experimental.pallas import tpu as pltpu

F32 = jnp.float32
BF16 = jnp.bfloat16

CONV_W = 3
CMP_BLOCK = 32
SEL_BLOCK = 64
N_SEL = 16
WINDOW = 512
ROPE_THETA = 10000.0
LN_EPS = 1e-5
NEG = -1e30
FORCE = 1e4

LANES = 128
SUBLANES = 8
VMEM_LIMIT_BYTES = 56 * 1024 * 1024


def _params(*sem):
    return pltpu.CompilerParams(dimension_semantics=sem, vmem_limit_bytes=VMEM_LIMIT_BYTES)


def _tile(n, pref, align):
    if n <= pref:
        return n
    t = (pref // align) * align
    while t >= align:
        if n % t == 0:
            return t
        t -= align
    raise ValueError(f"no tile for {n} (pref {pref}, align {align})")


def _dot(a, b):
    return jnp.dot(a, b, preferred_element_type=F32)


def _dot_nt(a, b):
    return lax.dot_general(a, b, (((1,), (1,)), ((), ())), preferred_element_type=F32)


def _sigmoid(x):
    return 1.0 / (1.0 + jnp.exp(-x))


def _rope_head(a, cos, sin_signed):
    return a * cos + pltpu.roll(a, a.shape[-1] // 2, axis=1) * sin_signed


def _layer_norm(z, g, b):
    mu = jnp.mean(z, axis=-1, keepdims=True)
    zc = z - mu
    var = jnp.mean(zc * zc, axis=-1, keepdims=True)
    return zc * lax.rsqrt(var + LN_EPS) * g + b


def _conv_in_body(x_ref, wb_ref, wc_ref, wh_ref, cw_ref, prev_ref, v_ref, st_ref, conv_ref, *, seq, nseq):
    x = x_ref[...]
    b = _dot(x, wb_ref[...])
    u = _dot(x, wc_ref[...]) * _dot(x, wh_ref[...])
    w0 = cw_ref[0:1, :]
    w1 = cw_ref[1:2, :]
    w2 = cw_ref[2:3, :]
    for s in range(nseq):
        us = u[s * seq:(s + 1) * seq]
        tt = lax.broadcasted_iota(jnp.int32, us.shape, 0)
        p0 = prev_ref[s, 0:1, :]
        p1 = prev_ref[s, 1:2, :]
        u1 = jnp.where(tt >= 1, pltpu.roll(us, 1, axis=0), p1)
        u2 = jnp.where(tt >= 2, pltpu.roll(us, 2, axis=0), jnp.where(tt == 0, p0, p1))
        conv_ref[s * seq:(s + 1) * seq, :] = u2 * w0 + u1 * w1 + us * w2
        st_ref[s] = us[seq - SUBLANES:seq]
    v_ref[...] = (b * conv_ref[...]).astype(v_ref.dtype)


def _conv_in(xb, w_in, conv_w, prev, seq):
    m, d = xb.shape
    nb_seq = m // seq
    nseq = 1 if seq >= 512 else nb_seq
    bm = nseq * seq
    bn = _tile(d, 256, LANES)
    nb = d // bn
    x_mode = pl.Buffered(1) if bm * d * 2 > (8 << 20) else None
    return pl.pallas_call(
        functools.partial(_conv_in_body, seq=seq, nseq=nseq),
        grid=(m // bm, nb),
        in_specs=[
            pl.BlockSpec((bm, d), lambda i, j: (i, 0), pipeline_mode=x_mode),
            pl.BlockSpec((d, bn), lambda i, j: (0, j)),
            pl.BlockSpec((d, bn), lambda i, j: (0, j + nb)),
            pl.BlockSpec((d, bn), lambda i, j: (0, j + 2 * nb)),
            pl.BlockSpec((CONV_W, bn), lambda i, j: (0, j)),
            pl.BlockSpec((nseq, CONV_W - 1, bn), lambda i, j: (i, 0, j)),
        ],
        out_specs=[
            pl.BlockSpec((bm, bn), lambda i, j: (i, j)),
            pl.BlockSpec((nseq, SUBLANES, bn), lambda i, j: (i, 0, j)),
        ],
        out_shape=[
            jax.ShapeDtypeStruct((m, d), BF16),
            jax.ShapeDtypeStruct((nb_seq, SUBLANES, d), F32),
        ],
        scratch_shapes=[pltpu.VMEM((bm, bn), F32)],
        compiler_params=_params("parallel", "arbitrary"),
        name="conv_in",
    )(xb, w_in, w_in, w_in, conv_w, prev)


def _proj_ln_body(lhs_ref, w_ref, res_ref, g_ref, b_ref, o_ref, ob_ref, *, nk, alpha, bn, br):
    k = pl.program_id(1)
    bm, d = o_ref.shape

    @pl.when(k == 0)
    def _():
        o_ref[...] = jnp.zeros_like(o_ref)

    lhs = lhs_ref[...]
    for c in range(d // bn):
        o_ref[:, c * bn:(c + 1) * bn] += _dot(lhs, w_ref[:, c * bn:(c + 1) * bn])

    @pl.when(k == nk - 1)
    def _():
        def rows(r, carry):
            sl = pl.ds(pl.multiple_of(r * br, br), br)
            y = _layer_norm(alpha * res_ref[sl, :] + o_ref[sl, :], g_ref[...], b_ref[...])
            o_ref[sl, :] = y
            ob_ref[sl, :] = y.astype(ob_ref.dtype)
            return carry

        lax.fori_loop(0, bm // br, rows, 0)


def _proj_ln(lhs, w, res, g, b, alpha):
    e, m, kdim = lhs.shape
    d = w.shape[-1]
    bm = _tile(m, 512, SUBLANES * 2)
    bk = _tile(kdim, 512, LANES)
    nkk = kdim // bk
    nk = e * nkk
    bn = _tile(d, 512, LANES)
    br = _tile(bm, 32, SUBLANES * 2)
    return pl.pallas_call(
        functools.partial(_proj_ln_body, nk=nk, alpha=alpha, bn=bn, br=br),
        grid=(m // bm, nk),
        in_specs=[
            pl.BlockSpec((None, bm, bk), lambda i, k: (k // nkk, i, k % nkk)),
            pl.BlockSpec((None, bk, d), lambda i, k: (k // nkk, k % nkk, 0)),
            pl.BlockSpec((bm, d), lambda i, k: (i, 0), pipeline_mode=pl.Buffered(1)),
            pl.BlockSpec((1, d), lambda i, k: (0, 0)),
            pl.BlockSpec((1, d), lambda i, k: (0, 0)),
        ],
        out_specs=[
            pl.BlockSpec((bm, d), lambda i, k: (i, 0)),
            pl.BlockSpec((bm, d), lambda i, k: (i, 0)),
        ],
        out_shape=[jax.ShapeDtypeStruct((m, d), F32), jax.ShapeDtypeStruct((m, d), BF16)],
        compiler_params=_params("parallel", "arbitrary"),
        name="proj_ln",
    )(lhs, w, res, g.reshape(1, d), b.reshape(1, d))


def _swiglu_body(x_ref, wg_ref, wu_ref, gate_ref, h_ref, *, gated):
    x = x_ref[...]
    g = _dot(x, wg_ref[...])
    u = _dot(x, wu_ref[...])
    h = g * _sigmoid(g) * u
    if gated:
        h = h * gate_ref[...]
    h_ref[...] = h.astype(h_ref.dtype)


def _swiglu(xb, w_gu, f, gates_e):
    m, d = xb.shape
    e = w_gu.shape[0]
    bm = _tile(m, 1024, SUBLANES * 2)
    bn = _tile(f, 512, LANES)
    nb = f // bn
    gated = gates_e is not None
    if not gated:
        gates_e = jnp.ones((e, m, 1), F32)
    return pl.pallas_call(
        functools.partial(_swiglu_body, gated=gated),
        grid=(e, m // bm, nb),
        in_specs=[
            pl.BlockSpec((bm, d), lambda ee, i, j: (i, 0)),
            pl.BlockSpec((None, d, bn), lambda ee, i, j: (ee, 0, j)),
            pl.BlockSpec((None, d, bn), lambda ee, i, j: (ee, 0, j + nb)),
            pl.BlockSpec((None, bm, 1), lambda ee, i, j: (ee, i, 0)),
        ],
        out_specs=pl.BlockSpec((None, bm, bn), lambda ee, i, j: (ee, i, j)),
        out_shape=jax.ShapeDtypeStruct((e, m, f), BF16),
        compiler_params=_params("parallel", "parallel", "arbitrary"),
        name="swiglu",
    )(xb, w_gu, w_gu, gates_e)


def _router_body(x_ref, w_ref, b_ref, o_ref, *, n_experts):
    logits = jnp.dot(x_ref[...], w_ref[...], preferred_element_type=F32,
                     precision=lax.Precision.HIGHEST) + b_ref[...]
    lane = lax.broadcasted_iota(jnp.int32, logits.shape, 1)
    lg = jnp.where(lane < n_experts, logits, -jnp.inf)
    m1 = jnp.max(lg, axis=-1, keepdims=True)
    i1 = jnp.min(jnp.where(lg == m1, lane, LANES), axis=-1, keepdims=True)
    lg2 = jnp.where(lane == i1, -jnp.inf, lg)
    m2 = jnp.max(lg2, axis=-1, keepdims=True)
    i2 = jnp.min(jnp.where(lg2 == m2, lane, LANES), axis=-1, keepdims=True)
    e2 = jnp.exp(m2 - m1)
    w1 = 1.0 / (1.0 + e2)
    w2 = e2 / (1.0 + e2)
    o_ref[...] = jnp.where(lane == i1, w1, 0.0) + jnp.where(lane == i2, w2, 0.0)


def _router(x, w_router, b_router):
    m, d = x.shape
    n_experts = w_router.shape[-1]
    w = jnp.pad(w_router, ((0, 0), (0, LANES - n_experts)))
    b = jnp.pad(b_router, (0, LANES - n_experts)).reshape(1, LANES)
    bm = _tile(m, 512, SUBLANES)
    return pl.pallas_call(
        functools.partial(_router_body, n_experts=n_experts),
        grid=(m // bm,),
        in_specs=[
            pl.BlockSpec((bm, d), lambda i: (i, 0)),
            pl.BlockSpec((d, LANES), lambda i: (0, 0)),
            pl.BlockSpec((1, LANES), lambda i: (0, 0)),
        ],
        out_specs=pl.BlockSpec((bm, LANES), lambda i: (i, 0)),
        out_shape=jax.ShapeDtypeStruct((m, LANES), F32),
        compiler_params=_params("parallel"),
        name="router",
    )(x, w, b)


def _ple_body(xb_ref, x_ref, p_ref, wp_ref, wg_ref, o_ref, ob_ref):
    gate = _sigmoid(_dot(xb_ref[...], wg_ref[...]))
    y = x_ref[...] + _dot(p_ref[...], wp_ref[...]) * gate
    o_ref[...] = y
    ob_ref[...] = y.astype(ob_ref.dtype)


def _ple(x, xb, pb, w_proj, w_gate):
    m, d = x.shape
    dp = pb.shape[-1]
    bm = _tile(m, 1024, SUBLANES * 2)
    bn = _tile(d, 512, LANES)
    return pl.pallas_call(
        _ple_body,
        grid=(m // bm, d // bn),
        in_specs=[
            pl.BlockSpec((bm, d), lambda i, j: (i, 0)),
            pl.BlockSpec((bm, bn), lambda i, j: (i, j)),
            pl.BlockSpec((bm, dp), lambda i, j: (i, 0)),
            pl.BlockSpec((dp, bn), lambda i, j: (0, j)),
            pl.BlockSpec((d, bn), lambda i, j: (0, j)),
        ],
        out_specs=[
            pl.BlockSpec((bm, bn), lambda i, j: (i, j)),
            pl.BlockSpec((bm, bn), lambda i, j: (i, j)),
        ],
        out_shape=[jax.ShapeDtypeStruct((m, d), F32), jax.ShapeDtypeStruct((m, d), BF16)],
        compiler_params=_params("parallel", "arbitrary"),
        name="ple",
    )(xb, x, pb, w_proj, w_gate)


def _heads_body(x_ref, w_ref, cos_ref, sin_ref, o_ref, or_ref, *, heads_per_blk, rope_blocks, two_out):
    acc = _dot(x_ref[...], w_ref[...])
    cos = cos_ref[...]
    sin = sin_ref[...]

    def roped():
        return [_rope_head(acc[:, h * LANES:(h + 1) * LANES], cos, sin) for h in range(heads_per_blk)]

    if two_out:
        o_ref[...] = acc.astype(o_ref.dtype)
        for h, y in enumerate(roped()):
            or_ref[:, h * LANES:(h + 1) * LANES] = y.astype(or_ref.dtype)
    else:
        j = pl.program_id(1)
        is_rope = functools.reduce(jnp.logical_or, [j == rb for rb in rope_blocks])

        @pl.when(is_rope)
        def _():
            for h, y in enumerate(roped()):
                o_ref[:, h * LANES:(h + 1) * LANES] = y
                or_ref[:, h * LANES:(h + 1) * LANES] = y.astype(or_ref.dtype)

        @pl.when(jnp.logical_not(is_rope))
        def _():
            o_ref[...] = acc
            or_ref[...] = acc.astype(or_ref.dtype)


def _heads_proj(xb, w, cos, sin, *, two_out, rope_blocks, out_dtypes):
    m, d = xb.shape
    n = w.shape[-1]
    bn = 4 * LANES
    bm = _tile(m, 1024, SUBLANES * 2)
    return pl.pallas_call(
        functools.partial(_heads_body, heads_per_blk=bn // LANES, rope_blocks=rope_blocks, two_out=two_out),
        grid=(m // bm, n // bn),
        in_specs=[
            pl.BlockSpec((bm, d), lambda i, j: (i, 0)),
            pl.BlockSpec((d, bn), lambda i, j: (0, j)),
            pl.BlockSpec((bm, LANES), lambda i, j: (i, 0)),
            pl.BlockSpec((bm, LANES), lambda i, j: (i, 0)),
        ],
        out_specs=[
            pl.BlockSpec((bm, bn), lambda i, j: (i, j)),
            pl.BlockSpec((bm, bn), lambda i, j: (i, j)),
        ],
        out_shape=[jax.ShapeDtypeStruct((m, n), out_dtypes[0]), jax.ShapeDtypeStruct((m, n), out_dtypes[1])],
        compiler_params=_params("parallel", "arbitrary"),
        name="heads_proj",
    )(xb, w, cos, sin)


def _gates_body(x_ref, w_ref, o_ref):
    o_ref[...] = _sigmoid(_dot(x_ref[...], w_ref[...]))


def _gates_proj(xb, w):
    m, d = xb.shape
    n = w.shape[-1]
    bm = _tile(m, 1024, SUBLANES * 2)
    return pl.pallas_call(
        _gates_body,
        grid=(m // bm,),
        in_specs=[pl.BlockSpec((bm, d), lambda i: (i, 0)), pl.BlockSpec((d, n), lambda i: (0, 0))],
        out_specs=pl.BlockSpec((bm, n), lambda i: (i, 0)),
        out_shape=jax.ShapeDtypeStruct((m, n), F32),
        compiler_params=_params("parallel"),
        name="gates_proj",
    )(xb, w)


def _gather_body(pt_ref, x_ref, oc_ref, os_ref):
    half = oc_ref.shape[-1]
    oc_ref[...] = x_ref[:, :half]
    os_ref[...] = x_ref[:, half:].astype(os_ref.dtype)


def _gather_pages(cache2, page_table):
    n_pool, page, width = cache2.shape
    nb, n_pages = page_table.shape
    half = width // 2
    grid_spec = pltpu.PrefetchScalarGridSpec(
        num_scalar_prefetch=1,
        grid=(nb, n_pages),
        in_specs=[pl.BlockSpec((None, page, width), lambda b, p, pt: (pt[b, p], 0, 0))],
        out_specs=[
            pl.BlockSpec((None, page, half), lambda b, p, pt: (b, p, 0)),
            pl.BlockSpec((None, page, half), lambda b, p, pt: (b, p, 0)),
        ],
    )
    return pl.pallas_call(
        _gather_body,
        grid_spec=grid_spec,
        out_shape=[
            jax.ShapeDtypeStruct((nb, n_pages * page, half), F32),
            jax.ShapeDtypeStruct((nb, n_pages * page, half), BF16),
        ],
        compiler_params=_params("parallel", "arbitrary"),
        name="gather_pages",
    )(page_table, cache2)


def _compress_body(x_ref, pe_ref, w_ref, o_ref, *, nc):
    half = nc // 2
    acc = jnp.zeros((nc, LANES), F32)
    for l in range(CMP_BLOCK):
        xe = x_ref[pl.ds(l, half, stride=2 * CMP_BLOCK), :]
        xo = x_ref[pl.ds(l + CMP_BLOCK, half, stride=2 * CMP_BLOCK), :]
        xl = jnp.concatenate([xe, xo], axis=0) + pe_ref[l:l + 1, :]
        acc = acc + _dot(xl.astype(BF16), w_ref[l])
    o_ref[...] = acc.astype(o_ref.dtype)


def _compress(x3, pe, w, n_kv, nc):
    nb = x3.shape[0]
    tc = nc * CMP_BLOCK
    return pl.pallas_call(
        functools.partial(_compress_body, nc=nc),
        grid=(nb, 2, n_kv),
        in_specs=[
            pl.BlockSpec((None, tc, LANES), lambda b, c, g: (b, 0, c * n_kv + g)),
            pl.BlockSpec((None, CMP_BLOCK, LANES), lambda b, c, g: (c, 0, 0)),
            pl.BlockSpec((None, CMP_BLOCK, LANES, LANES), lambda b, c, g: (c, 0, 0, 0)),
        ],
        out_specs=pl.BlockSpec((None, None, None, nc, LANES), lambda b, c, g: (c, b, g, 0, 0)),
        out_shape=jax.ShapeDtypeStruct((2, nb, n_kv, nc, LANES), BF16),
        compiler_params=_params("parallel", "parallel", "arbitrary"),
        name="compress",
    )(x3, pe, w)


def _cmp_block_of_lane(nc, shape):
    lane = lax.broadcasted_iota(jnp.int32, shape, len(shape) - 1)
    half = nc // 2
    return jnp.where(lane < half, 2 * lane, 2 * (lane - half) + 1)


def _cmp_branch(q, kc, vc, trow, nc, scale):
    s = _dot_nt(q, kc) * scale
    n = _cmp_block_of_lane(nc, s.shape)
    ok = ((n + 1) * CMP_BLOCK - 1) <= trow
    sm = jnp.where(ok, s, NEG)
    e = jnp.exp(sm - jnp.max(sm, axis=-1, keepdims=True))
    p = jnp.where(ok, e / jnp.sum(e, axis=-1, keepdims=True), 0.0)
    return p, _dot(p.astype(BF16), vc)


def _online_update(carry, s, ok, v):
    m, l, acc = carry
    sm = jnp.where(ok, s, NEG)
    m_new = jnp.maximum(m, jnp.max(sm, axis=-1, keepdims=True))
    a = jnp.exp(m - m_new)
    p = jnp.where(ok, jnp.exp(sm - m_new), 0.0)
    return m_new, a * l + jnp.sum(p, axis=-1, keepdims=True), a * acc + _dot(p.astype(BF16), v)


def _online_init(rows):
    return (jnp.full((rows, 1), NEG, F32), jnp.zeros((rows, 1), F32), jnp.zeros((rows, LANES), F32))


def _block_mask(sel_rows, k0, tk):
    nbk = sel_rows.shape[-1]
    blk = lax.broadcasted_iota(jnp.int32, (nbk, tk), 0)
    kpos = k0 + lax.broadcasted_iota(jnp.int32, (nbk, tk), 1)
    expand = (blk == kpos // SEL_BLOCK).astype(BF16)
    return _dot(sel_rows, expand)


def _attn_prompt_body(q_ref, qr_ref, gt_ref, kc_ref, vc_ref, ks_ref, vs_ref, kw_ref, vw_ref, o_ref,
                      *, n_rep, tq, seq, nc, nsb, scale):
    t0 = pl.program_id(2) * tq
    rows = n_rep * tq
    q = jnp.concatenate([q_ref[:, r * LANES:(r + 1) * LANES] for r in range(n_rep)], axis=0)
    qr = jnp.concatenate([qr_ref[:, r * LANES:(r + 1) * LANES] for r in range(n_rep)], axis=0)
    trow = t0 + lax.rem(lax.broadcasted_iota(jnp.int32, (rows, 1), 0), tq)

    p_cmp, o_cmp = _cmp_branch(q, kc_ref[...], vc_ref[...], trow, nc, scale)
    imp = p_cmp[0:tq]
    for r in range(1, n_rep):
        imp = imp + p_cmp[r * tq:(r + 1) * tq]
    half = nc // 2
    imp = imp[:, :half] + imp[:, half:nc]
    imp = jnp.concatenate([imp, jnp.zeros((tq, LANES - half), F32)], axis=1)
    imp_t = imp.T[:nsb]

    j = lax.broadcasted_iota(jnp.int32, (nsb, tq), 0)
    tcol = t0 + lax.broadcasted_iota(jnp.int32, (nsb, tq), 1)
    cur = tcol // SEL_BLOCK
    forced = (j == 0) | (j == cur) | (j == cur - 1)
    avail = j * SEL_BLOCK <= tcol
    work = jnp.where(avail, imp_t + FORCE * forced.astype(F32), -1.0)
    sel = jnp.zeros((nsb, tq), F32)
    for _ in range(min(N_SEL, nsb)):
        mx = jnp.max(work, axis=0, keepdims=True)
        idx = jnp.min(jnp.where(work == mx, j, nsb), axis=0, keepdims=True)
        pick = j == idx
        sel = jnp.where(pick, 1.0, sel)
        work = jnp.where(pick, -jnp.inf, work)
    sel = jnp.where(avail, sel, 0.0)
    sel = jnp.concatenate([sel, jnp.zeros((LANES - nsb, tq), F32)], axis=0).T
    sel_rows = jnp.concatenate([sel.astype(BF16)] * n_rep, axis=0)

    tk = min(512, seq)
    n_tiles = (t0 + tq + tk - 1) // tk

    def sel_step(kt, carry):
        k0 = pl.multiple_of(kt * tk, tk)
        s = _dot_nt(qr, ks_ref[pl.ds(k0, tk), :]) * scale
        kpos = k0 + lax.broadcasted_iota(jnp.int32, (rows, tk), 1)
        ok = (_block_mask(sel_rows, k0, tk) > 0.5) & (kpos <= trow)
        return _online_update(carry, s, ok, vs_ref[pl.ds(k0, tk), :])

    _, l_sel, acc_sel = lax.fori_loop(0, n_tiles, sel_step, _online_init(rows))
    o_sel = acc_sel / l_sel

    wl = min(WINDOW + tq, seq)
    w0 = pl.multiple_of(jnp.maximum(t0 + tq - wl, 0), tq)
    s = _dot_nt(qr, kw_ref[pl.ds(w0, wl), :]) * scale
    dist = trow - (w0 + lax.broadcasted_iota(jnp.int32, (rows, wl), 1))
    ok = (dist >= 0) & (dist < WINDOW)
    _, l_win, acc_win = _online_update(_online_init(rows), s, ok, vw_ref[pl.ds(w0, wl), :])
    o_win = acc_win / l_win

    for r in range(n_rep):
        sl = slice(r * tq, (r + 1) * tq)
        o = (gt_ref[:, r:r + 1] * o_cmp[sl]
             + gt_ref[:, n_rep + r:n_rep + r + 1] * o_sel[sl]
             + gt_ref[:, 2 * n_rep + r:2 * n_rep + r + 1] * o_win[sl])
        o_ref[:, r * LANES:(r + 1) * LANES] = o.astype(o_ref.dtype)


def _attn_prompt(q, qr, gates_g, cmp_kv, kvb, nb, seq, n_kv):
    m, d = q.shape
    n_rep = d // (n_kv * LANES)
    nc = seq // CMP_BLOCK
    nsb = -(-seq // SEL_BLOCK)
    tq = _tile(seq, 128, LANES)
    nq = seq // tq
    hw = n_rep * LANES
    kv_spec = lambda part: pl.BlockSpec((seq, LANES), lambda b, g, i: (b, part * n_kv + g))
    cmp_spec = lambda c: pl.BlockSpec((None, None, None, nc, LANES), lambda b, g, i: (c, b, g, 0, 0))
    return pl.pallas_call(
        functools.partial(_attn_prompt_body, n_rep=n_rep, tq=tq, seq=seq, nc=nc, nsb=nsb,
                          scale=float(LANES) ** -0.5),
        grid=(nb, n_kv, nq),
        in_specs=[
            pl.BlockSpec((tq, hw), lambda b, g, i: (b * nq + i, g)),
            pl.BlockSpec((tq, hw), lambda b, g, i: (b * nq + i, g)),
            pl.BlockSpec((None, tq, 3 * n_rep), lambda b, g, i: (g, b * nq + i, 0)),
            cmp_spec(0), cmp_spec(1),
            kv_spec(2), kv_spec(3), kv_spec(4), kv_spec(5),
        ],
        out_specs=pl.BlockSpec((tq, hw), lambda b, g, i: (b * nq + i, g)),
        out_shape=jax.ShapeDtypeStruct((m, d), BF16),
        compiler_params=_params("parallel", "parallel", "arbitrary"),
        name="attn_prompt",
    )(q, qr, gates_g, cmp_kv, cmp_kv, kvb, kvb, kvb, kvb)


def _pad_rows(a, rows):
    return jnp.concatenate([a, jnp.zeros((rows - a.shape[0], a.shape[1]), a.dtype)], axis=0)


def _attn_sample_body(q_ref, qr_ref, gt_ref, kc_ref, vc_ref, ksp_ref, vsp_ref, ksn_ref, vsn_ref,
                      kwp_ref, vwp_ref, kwn_ref, vwn_ref, o_ref,
                      *, n_rep, tn, past, nc, nsb, nbk, scale):
    rows = n_rep * tn
    q = jnp.concatenate([q_ref[:, r * LANES:(r + 1) * LANES] for r in range(n_rep)], axis=0).astype(BF16)
    qr = jnp.concatenate([qr_ref[:, r * LANES:(r + 1) * LANES] for r in range(n_rep)], axis=0).astype(BF16)
    trow = past + lax.rem(lax.broadcasted_iota(jnp.int32, (rows, 1), 0), tn)

    p_cmp, o_cmp = _cmp_branch(q, kc_ref[...], vc_ref[...], trow, nc, scale)
    imp = p_cmp[0:tn]
    for r in range(1, n_rep):
        imp = imp + p_cmp[r * tn:(r + 1) * tn]
    half = nc // 2
    imp = imp[:, :half] + imp[:, half:nc]
    imp = jnp.concatenate([imp, jnp.zeros((tn, nbk - half), F32)], axis=1)

    j = lax.broadcasted_iota(jnp.int32, (tn, nbk), 1)
    tcol = past + lax.broadcasted_iota(jnp.int32, (tn, nbk), 0)
    cur = tcol // SEL_BLOCK
    forced = (j == 0) | (j == cur) | (j == cur - 1)
    avail = (j * SEL_BLOCK <= tcol) & (j < nsb)
    work = jnp.where(avail, imp + FORCE * forced.astype(F32), -1.0)
    work = jnp.where(j < nsb, work, -jnp.inf)
    sel = jnp.zeros((tn, nbk), F32)
    for _ in range(min(N_SEL, nsb)):
        mx = jnp.max(work, axis=1, keepdims=True)
        idx = jnp.min(jnp.where(work == mx, j, nbk), axis=1, keepdims=True)
        pick = j == idx
        sel = jnp.where(pick, 1.0, sel)
        work = jnp.where(pick, -jnp.inf, work)
    sel = jnp.where(avail, sel, 0.0)
    sel_rows = jnp.concatenate([sel.astype(BF16)] * n_rep, axis=0)

    tk = min(1024, past)

    def sel_step(kt, carry):
        k0 = pl.multiple_of(kt * tk, tk)
        s = _dot_nt(qr, ksp_ref[pl.ds(k0, tk), :]) * scale
        kpos = k0 + lax.broadcasted_iota(jnp.int32, (rows, tk), 1)
        ok = (_block_mask(sel_rows, k0, tk) > 0.5) & (kpos <= trow)
        return _online_update(carry, s, ok, vsp_ref[pl.ds(k0, tk), :])

    carry = lax.fori_loop(0, past // tk, sel_step, _online_init(rows))
    kn = _pad_rows(ksn_ref[...], LANES).astype(BF16)
    vn = _pad_rows(vsn_ref[...], LANES).astype(BF16)
    kpos = past + lax.broadcasted_iota(jnp.int32, (rows, LANES), 1)
    ok = (_block_mask(sel_rows, past, LANES) > 0.5) & (kpos <= trow)
    _, l_sel, acc_sel = _online_update(carry, _dot_nt(qr, kn) * scale, ok, vn)
    o_sel = acc_sel / l_sel

    wb = kwp_ref.shape[0]
    dist = trow - (past - wb + lax.broadcasted_iota(jnp.int32, (rows, wb), 1))
    ok = (dist >= 0) & (dist < WINDOW)
    carry = _online_update(_online_init(rows), _dot_nt(qr, kwp_ref[...].astype(BF16)) * scale, ok,
                           vwp_ref[...].astype(BF16))
    kn = _pad_rows(kwn_ref[...], LANES).astype(BF16)
    vn = _pad_rows(vwn_ref[...], LANES).astype(BF16)
    dist = trow - kpos
    ok = (dist >= 0) & (dist < WINDOW)
    _, l_win, acc_win = _online_update(carry, _dot_nt(qr, kn) * scale, ok, vn)
    o_win = acc_win / l_win

    for r in range(n_rep):
        sl = slice(r * tn, (r + 1) * tn)
        o = (gt_ref[:, r:r + 1] * o_cmp[sl]
             + gt_ref[:, n_rep + r:n_rep + r + 1] * o_sel[sl]
             + gt_ref[:, 2 * n_rep + r:2 * n_rep + r + 1] * o_win[sl])
        o_ref[:, r * LANES:(r + 1) * LANES] = o


def _attn_sample(q, qr, gates_g, cmp_kv, sel_past, kv_new, win_past, nb, tn, n_kv):
    m, d = q.shape
    n_rep = d // (n_kv * LANES)
    past = sel_past.shape[1]
    nc = cmp_kv.shape[3]
    nsb = -(-(past + tn) // SEL_BLOCK)
    nbk = -(-nsb // LANES) * LANES
    wb = win_past.shape[1]
    hw = n_rep * LANES
    new_spec = lambda part: pl.BlockSpec((tn, LANES), lambda b, g: (b, part * n_kv + g))
    cmp_spec = lambda c: pl.BlockSpec((None, None, None, nc, LANES), lambda b, g: (c, b, g, 0, 0))
    return pl.pallas_call(
        functools.partial(_attn_sample_body, n_rep=n_rep, tn=tn, past=past, nc=nc, nsb=nsb, nbk=nbk,
                          scale=float(LANES) ** -0.5),
        grid=(nb, n_kv),
        in_specs=[
            pl.BlockSpec((tn, hw), lambda b, g: (b, g)),
            pl.BlockSpec((tn, hw), lambda b, g: (b, g)),
            pl.BlockSpec((None, tn, 3 * n_rep), lambda b, g: (g, b, 0)),
            cmp_spec(0), cmp_spec(1),
            pl.BlockSpec((None, past, LANES), lambda b, g: (b, 0, g)),
            pl.BlockSpec((None, past, LANES), lambda b, g: (b, 0, n_kv + g)),
            new_spec(2), new_spec(3),
            pl.BlockSpec((None, wb, LANES), lambda b, g: (b, 0, g)),
            pl.BlockSpec((None, wb, LANES), lambda b, g: (b, 0, n_kv + g)),
            new_spec(4), new_spec(5),
        ],
        out_specs=pl.BlockSpec((tn, hw), lambda b, g: (b, g)),
        out_shape=jax.ShapeDtypeStruct((m, d), F32),
        compiler_params=_params("parallel", "arbitrary"),
        name="attn_sample",
    )(q, qr, gates_g, cmp_kv, cmp_kv, sel_past, sel_past, kv_new, kv_new, win_past, win_past, kv_new, kv_new)


def _rope_tables(pos):
    half = LANES // 2
    inv = ROPE_THETA ** (-jnp.arange(half, dtype=F32) / half)
    ang = pos.astype(F32)[:, None] * inv[None, :]
    cos, sin = jnp.cos(ang), jnp.sin(ang)
    return jnp.concatenate([cos, cos], axis=-1), jnp.concatenate([-sin, sin], axis=-1)


def _prep_weights(a_w_in, a_w_out, w_kv, cmp_pe, cmp_w, b_w_qg, b_w_o, ffn_w_gu, ffn_w_down,
                  moe_w_gu, moe_w_down, ple_w_proj, ple_w_gate, n_kv):
    d = a_w_in.shape[1]
    n_heads = d // LANES
    n_rep = n_heads // n_kv
    f = ffn_w_down.shape[1]
    fp = -(-f // 512) * 512
    padf = ((0, 0), (0, 0), (0, fp - f))
    w = {}
    w["a_w_in"] = a_w_in.astype(BF16)
    w["a_w_out"] = a_w_out.astype(BF16)
    w["w_kv"] = w_kv.astype(BF16)
    w["cmp_pe"] = jnp.transpose(cmp_pe, (1, 0, 2))
    w["cmp_w"] = jnp.transpose(cmp_w, (1, 0, 2, 3)).astype(BF16)
    w["w_q"] = b_w_qg[:, :, :d].astype(BF16)
    wg = b_w_qg[:, :, d:].reshape(-1, d, n_kv, n_rep, 3)
    wg = jnp.transpose(wg, (0, 1, 2, 4, 3)).reshape(-1, d, 3 * n_heads)
    w["w_gates"] = jnp.pad(wg, ((0, 0), (0, 0), (0, LANES - 3 * n_heads))).astype(BF16)
    w["b_w_o"] = b_w_o.astype(BF16)
    w["ffn_w_gu"] = jnp.concatenate([jnp.pad(ffn_w_gu[:, :, :f], padf), jnp.pad(ffn_w_gu[:, :, f:], padf)],
                                    axis=-1).astype(BF16)
    w["ffn_w_down"] = jnp.pad(ffn_w_down, ((0, 0), (0, fp - f), (0, 0))).astype(BF16)
    w["ffn_f"] = fp
    w["moe_w_gu"] = moe_w_gu.astype(BF16)
    w["moe_w_down"] = moe_w_down.astype(BF16)
    w["ple_w_proj"] = ple_w_proj.astype(BF16)
    w["ple_w_gate"] = ple_w_gate.astype(BF16)
    return w


def _trunk(x3, p4, conv_prev, past_ctx, w, raw, n_kv):
    nb, seq, d = x3.shape
    m = nb * seq
    depth = raw["ln_g"].shape[0]
    n_a = depth // 2
    alpha = float((2 * depth) ** 0.25)
    n_heads = d // LANES
    n_rep = n_heads // n_kv
    hd_blk = n_kv * LANES
    past = 0 if past_ctx is None else past_ctx["past"]
    pos = past + jnp.arange(seq, dtype=jnp.int32)
    cos1, sin1 = _rope_tables(pos)
    cos = jnp.tile(cos1, (nb, 1))
    sin = jnp.tile(sin1, (nb, 1))

    x = x3.reshape(m, d)
    xb = x.astype(BF16)
    pb = p4.reshape(depth, m, -1).astype(BF16)
    conv_states = []
    ctx = None
    for i in range(depth):
        ln_g, ln_b = raw["ln_g"][i], raw["ln_b"][i]
        if i < n_a:
            prev = jnp.zeros((nb, CONV_W - 1, d), F32) if conv_prev is None else conv_prev[i]
            v, st = _conv_in(xb, w["a_w_in"][i], raw["a_conv_w"][i], prev, seq)
            conv_states.append(st[:, SUBLANES - (CONV_W - 1):])
            x, xb = _proj_ln(v[None], w["a_w_out"][i][None], x, ln_g[0], ln_b[0], alpha)
        else:
            li = i - n_a
            qdt = BF16 if past_ctx is None else F32
            q, qr = _heads_proj(xb, w["w_q"][li], cos, sin, two_out=True, rope_blocks=(), out_dtypes=(qdt, qdt))
            gates = _gates_proj(xb, w["w_gates"][li])[:, :3 * n_heads]
            gates_g = jnp.transpose(gates.reshape(m, n_kv, 3 * n_rep), (1, 0, 2))
            if past_ctx is None:
                o = _attn_prompt(q, qr, gates_g, ctx["cmp_kv"], ctx["kvb"], nb, seq, n_kv)
            else:
                o = _attn_sample(q, qr, gates_g, ctx["cmp_kv"], past_ctx["sel_past"], ctx["kv_all"],
                                 past_ctx["win_past"], nb, seq, n_kv).astype(BF16)
            x, xb = _proj_ln(o[None], w["b_w_o"][li][None], x, ln_g[0], ln_b[0], alpha)
        if i % 2 == 0:
            h = _swiglu(xb, w["ffn_w_gu"][i // 2][None], w["ffn_f"], None)
            x, xb = _proj_ln(h, w["ffn_w_down"][i // 2][None], x, ln_g[1], ln_b[1], alpha)
        else:
            n_exp = raw["moe_w_router"].shape[-1]
            gates_e = _router(x, raw["moe_w_router"][i // 2], raw["moe_b_router"][i // 2])
            gates_e = jnp.transpose(gates_e[:, :n_exp]).reshape(n_exp, m, 1)
            h = _swiglu(xb, w["moe_w_gu"][i // 2], w["moe_w_down"].shape[2], gates_e)
            x, xb = _proj_ln(h, w["moe_w_down"][i // 2], x, ln_g[1], ln_b[1], alpha)
        x, xb = _ple(x, xb, pb[i], w["ple_w_proj"][i], w["ple_w_gate"][i])
        if i == n_a - 1:
            kv_all, kvb = _heads_proj(xb, w["w_kv"], cos, sin, two_out=False, rope_blocks=(2, 4),
                                      out_dtypes=(F32, BF16))
            if past_ctx is None:
                nc = seq // CMP_BLOCK
                cmp_kv = _compress(kv_all.reshape(nb, seq, -1), w["cmp_pe"], w["cmp_w"], n_kv, nc)
            else:
                assert past % CMP_BLOCK == 0 and seq < CMP_BLOCK
                cmp_kv = _compress(past_ctx["cmp_past"], w["cmp_pe"], w["cmp_w"], n_kv, past // CMP_BLOCK)
            ctx = dict(kv_all=kv_all, kvb=kvb, cmp_kv=cmp_kv)
    rows = ctx["kv_all"][:, :4 * hd_blk].reshape(nb, seq, 4, n_kv, LANES)
    win = ctx["kv_all"][:, 4 * hd_blk:].reshape(nb, seq, 2, n_kv, LANES)
    return x.reshape(nb, seq, d), rows, win, jnp.stack(conv_states)


def kernel(x_prompt, x_sample, cache_kv, cache_win, state_conv, page_table, p_prompt, p_sample,
           a_w_in, a_conv_w, a_w_out, w_kv, cmp_pe, cmp_w, b_w_qg, b_w_o, ffn_w_gu, ffn_w_down,
           moe_w_router, moe_b_router, moe_w_gu, moe_w_down, ple_w_proj, ple_w_gate, ln_g, ln_b):
    n_kv = cache_kv.shape[3]
    raw = dict(a_conv_w=a_conv_w, moe_w_router=moe_w_router, moe_b_router=moe_b_router, ln_g=ln_g, ln_b=ln_b)
    w = _prep_weights(a_w_in, a_w_out, w_kv, cmp_pe, cmp_w, b_w_qg, b_w_o, ffn_w_gu, ffn_w_down,
                      moe_w_gu, moe_w_down, ple_w_proj, ple_w_gate, n_kv)

    y_p, rows_p, win_p, conv_p = _trunk(x_prompt, p_prompt, None, None, w, raw, n_kv)
    seq_p = x_prompt.shape[1]
    win_state_p = win_p[:, -min(WINDOW, seq_p):]

    n_pool, page = cache_kv.shape[:2]
    cmp_past, sel_past = _gather_pages(cache_kv.reshape(n_pool, page, -1), page_table)
    nb_s, wb = cache_win.shape[:2]
    past_ctx = dict(past=page_table.shape[1] * page, cmp_past=cmp_past, sel_past=sel_past,
                    win_past=cache_win.reshape(nb_s, wb, -1))
    y_s, rows_s, win_s, conv_s = _trunk(x_sample, p_sample, state_conv, past_ctx, w, raw, n_kv)
    win_state_s = jnp.concatenate([cache_win.astype(win_s.dtype), win_s], axis=1)[:, -wb:]

    return (y_p, y_s, rows_p, rows_s, win_state_p, win_state_s, conv_p, conv_s)
```

```python
import functools

import jax
import jax.numpy as jnp
from jax import lax
from jax.experimental import pallas as pl
from jax.experimental.pallas import tpu as pltpu

F32 = jnp.float32
BF16 = jnp.bfloat16

CONV_W = 3
CMP_BLOCK = 32
SEL_BLOCK = 64
N_SEL = 16
WINDOW = 512
ROPE_THETA = 10000.0
LN_EPS = 1e-5
NEG = -1e30
FORCE = 1e4

LANES = 128
SUBLANES = 8
VMEM_LIMIT_BYTES = 56 * 1024 * 1024


def _params(*sem):
    return pltpu.CompilerParams(dimension_semantics=sem, vmem_limit_bytes=VMEM_LIMIT_BYTES)


def _tile(n, pref, align):
    if n <= pref:
        return n
    t = (pref // align) * align
    while t >= align:
        if n % t == 0:
            return t
        t -= align
    raise ValueError(f"no tile for {n} (pref {pref}, align {align})")


def _dot(a, b):
    return jnp.dot(a, b, preferred_element_type=F32)


def _dot_nt(a, b):
    return lax.dot_general(a, b, (((1,), (1,)), ((), ())), preferred_element_type=F32)


def _sigmoid(x):
    return 1.0 / (1.0 + jnp.exp(-x))


def _rope_head(a, cos, sin_signed):
    return a * cos + pltpu.roll(a, a.shape[-1] // 2, axis=1) * sin_signed


def _layer_norm(z, g, b):
    mu = jnp.mean(z, axis=-1, keepdims=True)
    zc = z - mu
    var = jnp.mean(zc * zc, axis=-1, keepdims=True)
    return zc * lax.rsqrt(var + LN_EPS) * g + b


def _conv_in_body(x_ref, wb_ref, wc_ref, wh_ref, cw_ref, prev_ref, v_ref, st_ref, conv_ref, *, seq, nseq):
    x = x_ref[...]
    b = _dot(x, wb_ref[...])
    u = _dot(x, wc_ref[...]) * _dot(x, wh_ref[...])
    w0 = cw_ref[0:1, :]
    w1 = cw_ref[1:2, :]
    w2 = cw_ref[2:3, :]
    for s in range(nseq):
        us = u[s * seq:(s + 1) * seq]
        tt = lax.broadcasted_iota(jnp.int32, us.shape, 0)
        p0 = prev_ref[s, 0:1, :]
        p1 = prev_ref[s, 1:2, :]
        u1 = jnp.where(tt >= 1, pltpu.roll(us, 1, axis=0), p1)
        u2 = jnp.where(tt >= 2, pltpu.roll(us, 2, axis=0), jnp.where(tt == 0, p0, p1))
        conv_ref[s * seq:(s + 1) * seq, :] = u2 * w0 + u1 * w1 + us * w2
        st_ref[s] = us[seq - SUBLANES:seq]
    v_ref[...] = (b * conv_ref[...]).astype(v_ref.dtype)


def _conv_in(xb, w_in, conv_w, prev, seq):
    m, d = xb.shape
    nb_seq = m // seq
    nseq = 1 if seq >= 512 else nb_seq
    bm = nseq * seq
    bn = _tile(d, 256, LANES)
    nb = d // bn
    x_mode = pl.Buffered(1) if bm * d * 2 > (8 << 20) else None
    return pl.pallas_call(
        functools.partial(_conv_in_body, seq=seq, nseq=nseq),
        grid=(m // bm, nb),
        in_specs=[
            pl.BlockSpec((bm, d), lambda i, j: (i, 0), pipeline_mode=x_mode),
            pl.BlockSpec((d, bn), lambda i, j: (0, j)),
            pl.BlockSpec((d, bn), lambda i, j: (0, j + nb)),
            pl.BlockSpec((d, bn), lambda i, j: (0, j + 2 * nb)),
            pl.BlockSpec((CONV_W, bn), lambda i, j: (0, j)),
            pl.BlockSpec((nseq, CONV_W - 1, bn), lambda i, j: (i, 0, j)),
        ],
        out_specs=[
            pl.BlockSpec((bm, bn), lambda i, j: (i, j)),
            pl.BlockSpec((nseq, SUBLANES, bn), lambda i, j: (i, 0, j)),
        ],
        out_shape=[
            jax.ShapeDtypeStruct((m, d), BF16),
            jax.ShapeDtypeStruct((nb_seq, SUBLANES, d), F32),
        ],
        scratch_shapes=[pltpu.VMEM((bm, bn), F32)],
        compiler_params=_params("parallel", "arbitrary"),
        name="conv_in",
    )(xb, w_in, w_in, w_in, conv_w, prev)


def _proj_ln_body(lhs_ref, w_ref, res_ref, g_ref, b_ref, o_ref, ob_ref, *, nk, alpha, bn, br):
    k = pl.program_id(1)
    bm, d = o_ref.shape

    @pl.when(k == 0)
    def _():
        o_ref[...] = jnp.zeros_like(o_ref)

    lhs = lhs_ref[...]
    for c in range(d // bn):
        o_ref[:, c * bn:(c + 1) * bn] += _dot(lhs, w_ref[:, c * bn:(c + 1) * bn])

    @pl.when(k == nk - 1)
    def _():
        def rows(r, carry):
            sl = pl.ds(pl.multiple_of(r * br, br), br)
            y = _layer_norm(alpha * res_ref[sl, :] + o_ref[sl, :], g_ref[...], b_ref[...])
            o_ref[sl, :] = y
            ob_ref[sl, :] = y.astype(ob_ref.dtype)
            return carry

        lax.fori_loop(0, bm // br, rows, 0)


def _proj_ln(lhs, w, res, g, b, alpha):
    e, m, kdim = lhs.shape
    d = w.shape[-1]
    bm = _tile(m, 512, SUBLANES * 2)
    bk = _tile(kdim, 512, LANES)
    nkk = kdim // bk
    nk = e * nkk
    bn = _tile(d, 512, LANES)
    br = _tile(bm, 32, SUBLANES * 2)
    return pl.pallas_call(
        functools.partial(_proj_ln_body, nk=nk, alpha=alpha, bn=bn, br=br),
        grid=(m // bm, nk),
        in_specs=[
            pl.BlockSpec((None, bm, bk), lambda i, k: (k // nkk, i, k % nkk)),
            pl.BlockSpec((None, bk, d), lambda i, k: (k // nkk, k % nkk, 0)),
            pl.BlockSpec((bm, d), lambda i, k: (i, 0), pipeline_mode=pl.Buffered(1)),
            pl.BlockSpec((1, d), lambda i, k: (0, 0)),
            pl.BlockSpec((1, d), lambda i, k: (0, 0)),
        ],
        out_specs=[
            pl.BlockSpec((bm, d), lambda i, k: (i, 0)),
            pl.BlockSpec((bm, d), lambda i, k: (i, 0)),
        ],
        out_shape=[jax.ShapeDtypeStruct((m, d), F32), jax.ShapeDtypeStruct((m, d), BF16)],
        compiler_params=_params("parallel", "arbitrary"),
        name="proj_ln",
    )(lhs, w, res, g.reshape(1, d), b.reshape(1, d))


def _swiglu_body(x_ref, wg_ref, wu_ref, h_ref):
    x = x_ref[...]
    g = _dot(x, wg_ref[...])
    u = _dot(x, wu_ref[...])
    h_ref[...] = (g * _sigmoid(g) * u).astype(h_ref.dtype)


def _swiglu(xb, w_gu, f):
    m, d = xb.shape
    bm = _tile(m, 1024, SUBLANES * 2)
    bn = _tile(f, 512, LANES)
    nb = f // bn
    return pl.pallas_call(
        _swiglu_body,
        grid=(m // bm, nb),
        in_specs=[
            pl.BlockSpec((bm, d), lambda i, j: (i, 0)),
            pl.BlockSpec((d, bn), lambda i, j: (0, j)),
            pl.BlockSpec((d, bn), lambda i, j: (0, j + nb)),
        ],
        out_specs=pl.BlockSpec((bm, bn), lambda i, j: (i, j)),
        out_shape=jax.ShapeDtypeStruct((m, f), BF16),
        compiler_params=_params("parallel", "arbitrary"),
        name="swiglu",
    )(xb, w_gu, w_gu)


def _router_body(x_ref, w_ref, b_ref, o_ref, *, n_experts):
    logits = jnp.dot(x_ref[...], w_ref[...], preferred_element_type=F32,
                     precision=lax.Precision.HIGHEST) + b_ref[...]
    lane = lax.broadcasted_iota(jnp.int32, logits.shape, 1)
    lg = jnp.where(lane < n_experts, logits, -jnp.inf)
    m1 = jnp.max(lg, axis=-1, keepdims=True)
    i1 = jnp.min(jnp.where(lg == m1, lane, LANES), axis=-1, keepdims=True)
    lg2 = jnp.where(lane == i1, -jnp.inf, lg)
    m2 = jnp.max(lg2, axis=-1, keepdims=True)
    i2 = jnp.min(jnp.where(lg2 == m2, lane, LANES), axis=-1, keepdims=True)
    e2 = jnp.exp(m2 - m1)
    w1 = 1.0 / (1.0 + e2)
    w2 = e2 / (1.0 + e2)
    o_ref[...] = jnp.where(lane == 0, w1, jnp.where(lane == 1, w2, jnp.where(
        lane == 2, i1.astype(F32), jnp.where(lane == 3, i2.astype(F32), 0.0))))


def _router(x, w_router, b_router):
    m, d = x.shape
    n_experts = w_router.shape[-1]
    w = jnp.pad(w_router, ((0, 0), (0, LANES - n_experts)))
    b = jnp.pad(b_router, (0, LANES - n_experts)).reshape(1, LANES)
    bm = _tile(m, 512, SUBLANES)
    return pl.pallas_call(
        functools.partial(_router_body, n_experts=n_experts),
        grid=(m // bm,),
        in_specs=[
            pl.BlockSpec((bm, d), lambda i: (i, 0)),
            pl.BlockSpec((d, LANES), lambda i: (0, 0)),
            pl.BlockSpec((1, LANES), lambda i: (0, 0)),
        ],
        out_specs=pl.BlockSpec((bm, LANES), lambda i: (i, 0)),
        out_shape=jax.ShapeDtypeStruct((m, LANES), F32),
        compiler_params=_params("parallel"),
        name="router",
    )(x, w, b)


MOE_ROW_TILE = 512


def _route_meta(eid, n_exp, bm):
    m = eid.shape[0]
    a = 2 * m
    n_tiles = -(-a // bm) + n_exp
    e_flat = eid.reshape(a)
    order = jnp.argsort(e_flat, stable=True).astype(jnp.int32)
    e_sorted = e_flat[order]
    counts = jnp.sum(e_flat[:, None] == jnp.arange(n_exp, dtype=jnp.int32)[None, :], axis=0).astype(jnp.int32)
    tiles_e = (counts + bm - 1) // bm
    tile_end = jnp.cumsum(tiles_e).astype(jnp.int32)
    tile_start = tile_end - tiles_e
    first = jnp.cumsum(counts).astype(jnp.int32) - counts
    pos_sorted = tile_start[e_sorted] * bm + (jnp.arange(a, dtype=jnp.int32) - first[e_sorted])
    row_token = jnp.zeros((n_tiles * bm,), jnp.int32).at[pos_sorted].set(order // 2)
    pos = jnp.zeros((a,), jnp.int32).at[order].set(pos_sorted)
    tile_expert = jnp.minimum(jnp.searchsorted(tile_end, jnp.arange(n_tiles, dtype=jnp.int32), side="right"),
                              n_exp - 1).astype(jnp.int32)
    return dict(row_token=row_token, pos=pos, tile_expert=tile_expert, n_used=tile_end[-1:], n_tiles=n_tiles)


def _row_copy(src_hbm, row, dst, dst_row, sem):
    return pltpu.make_async_copy(src_hbm.at[pl.ds(row, 1), :], dst.at[pl.ds(dst_row, 1), :], sem)


def _moe_gather_body(tok_ref, x_hbm, o_ref, buf, sem, *, bm):
    base = pl.program_id(0) * bm

    def issue(r, carry):
        _row_copy(x_hbm, tok_ref[base + r], buf, r, sem).start()
        return carry

    def drain(r, carry):
        _row_copy(x_hbm, 0, buf, r, sem).wait()
        return carry

    lax.fori_loop(0, bm, issue, 0)
    lax.fori_loop(0, bm, drain, 0)
    o_ref[...] = buf[...].astype(o_ref.dtype)


def _moe_gather(row_token, x_all):
    d = x_all.shape[1]
    rows = row_token.shape[0]
    bm = _tile(rows, 256, SUBLANES * 2)
    grid_spec = pltpu.PrefetchScalarGridSpec(
        num_scalar_prefetch=1,
        grid=(rows // bm,),
        in_specs=[pl.BlockSpec(memory_space=pl.ANY)],
        out_specs=pl.BlockSpec((bm, d), lambda i, tok: (i, 0)),
        scratch_shapes=[pltpu.VMEM((bm, d), F32), pltpu.SemaphoreType.DMA],
    )
    return pl.pallas_call(
        functools.partial(_moe_gather_body, bm=bm),
        grid_spec=grid_spec,
        out_shape=jax.ShapeDtypeStruct((rows, d), BF16),
        compiler_params=_params("arbitrary"),
        name="moe_gather",
    )(row_token, x_all)


def _moe_swiglu_body(te_ref, nu_ref, x_ref, wg_ref, wu_ref, h_ref, wcache, *, bn, bc):
    i = pl.program_id(1)
    changed = jnp.logical_or(i == 0, te_ref[i] != te_ref[jnp.maximum(i - 1, 0)])

    @pl.when(changed)
    def _():
        for c in range(wg_ref.shape[0] // bc):
            wcache[c * bc:(c + 1) * bc, :bn] = wg_ref[c * bc:(c + 1) * bc, :].astype(BF16)
            wcache[c * bc:(c + 1) * bc, bn:] = wu_ref[c * bc:(c + 1) * bc, :].astype(BF16)

    @pl.when(i < nu_ref[0])
    def _():
        gu = _dot(x_ref[...], wcache[...])
        g = gu[:, :bn]
        h_ref[...] = (g * _sigmoid(g) * gu[:, bn:]).astype(h_ref.dtype)

    @pl.when(i >= nu_ref[0])
    def _():
        h_ref[...] = jnp.zeros_like(h_ref)


def _moe_swiglu(meta, xs, w_gu, layer, bm):
    rows, d = xs.shape
    f = w_gu.shape[-1] // 2
    bn = _tile(f, 256, LANES)
    nb = f // bn
    grid_spec = pltpu.PrefetchScalarGridSpec(
        num_scalar_prefetch=2,
        grid=(nb, rows // bm),
        in_specs=[
            pl.BlockSpec((bm, d), lambda j, i, te, nu: (i, 0)),
            pl.BlockSpec((None, None, d, bn), lambda j, i, te, nu: (layer, te[i], 0, j)),
            pl.BlockSpec((None, None, d, bn), lambda j, i, te, nu: (layer, te[i], 0, j + nb)),
        ],
        out_specs=pl.BlockSpec((bm, bn), lambda j, i, te, nu: (i, j)),
        scratch_shapes=[pltpu.VMEM((d, 2 * bn), BF16)],
    )
    return pl.pallas_call(
        functools.partial(_moe_swiglu_body, bn=bn, bc=_tile(d, 512, SUBLANES * 2)),
        grid_spec=grid_spec,
        out_shape=jax.ShapeDtypeStruct((rows, f), BF16),
        compiler_params=_params("arbitrary", "arbitrary"),
        name="moe_swiglu",
    )(meta["tile_expert"], meta["n_used"], xs, w_gu, w_gu)


def _moe_down_body(te_ref, nu_ref, h_ref, w_ref, y_ref):
    i = pl.program_id(1)

    @pl.when(i < nu_ref[0])
    def _():
        y_ref[...] = _dot(h_ref[...], w_ref[...])

    @pl.when(i >= nu_ref[0])
    def _():
        y_ref[...] = jnp.zeros_like(y_ref)


def _moe_down(meta, h, w_down, bm):
    rows, f = h.shape
    d = w_down.shape[-1]
    bn = _tile(d, 512, LANES)
    grid_spec = pltpu.PrefetchScalarGridSpec(
        num_scalar_prefetch=2,
        grid=(d // bn, rows // bm),
        in_specs=[
            pl.BlockSpec((bm, f), lambda j, i, te, nu: (i, 0)),
            pl.BlockSpec((None, f, bn), lambda j, i, te, nu: (te[i], 0, j)),
        ],
        out_specs=pl.BlockSpec((bm, bn), lambda j, i, te, nu: (i, j)),
    )
    return pl.pallas_call(
        _moe_down_body,
        grid_spec=grid_spec,
        out_shape=jax.ShapeDtypeStruct((rows, d), F32),
        compiler_params=_params("parallel", "arbitrary"),
        name="moe_down",
    )(meta["tile_expert"], meta["n_used"], h, w_down)


def _moe_combine_body(pos_ref, y_hbm, w_ref, res_ref, g_ref, b_ref, o_ref, ob_ref, buf, sem, *, bm, br, alpha):
    base = pl.program_id(0) * bm

    def issue(r, carry):
        for k in range(2):
            _row_copy(y_hbm, pos_ref[2 * (base + r) + k], buf.at[k], r, sem).start()
        return carry

    def drain(r, carry):
        for k in range(2):
            _row_copy(y_hbm, 0, buf.at[k], r, sem).wait()
        return carry

    lax.fori_loop(0, bm, issue, 0)
    lax.fori_loop(0, bm, drain, 0)

    def rows(r, carry):
        sl = pl.ds(pl.multiple_of(r * br, br), br)
        f = w_ref[sl, 0:1] * buf[0, sl, :] + w_ref[sl, 1:2] * buf[1, sl, :]
        y = _layer_norm(alpha * res_ref[sl, :] + f, g_ref[...], b_ref[...])
        o_ref[sl, :] = y
        ob_ref[sl, :] = y.astype(ob_ref.dtype)
        return carry

    lax.fori_loop(0, bm // br, rows, 0)


def _moe_combine(pos, y, route, res, g, b, alpha):
    m, d = res.shape
    bm = _tile(m, 128, SUBLANES * 2)
    br = _tile(bm, 32, SUBLANES * 2)
    grid_spec = pltpu.PrefetchScalarGridSpec(
        num_scalar_prefetch=1,
        grid=(m // bm,),
        in_specs=[
            pl.BlockSpec(memory_space=pl.ANY),
            pl.BlockSpec((bm, LANES), lambda i, p: (i, 0)),
            pl.BlockSpec((bm, d), lambda i, p: (i, 0)),
            pl.BlockSpec((1, d), lambda i, p: (0, 0)),
            pl.BlockSpec((1, d), lambda i, p: (0, 0)),
        ],
        out_specs=[
            pl.BlockSpec((bm, d), lambda i, p: (i, 0)),
            pl.BlockSpec((bm, d), lambda i, p: (i, 0)),
        ],
        scratch_shapes=[pltpu.VMEM((2, bm, d), F32), pltpu.SemaphoreType.DMA],
    )
    return pl.pallas_call(
        functools.partial(_moe_combine_body, bm=bm, br=br, alpha=alpha),
        grid_spec=grid_spec,
        out_shape=[jax.ShapeDtypeStruct((m, d), F32), jax.ShapeDtypeStruct((m, d), BF16)],
        compiler_params=_params("arbitrary"),
        name="moe_combine",
    )(pos, y, route, res, g.reshape(1, d), b.reshape(1, d))


def _moe_layer(streams, layer, raw, w, ln_g, ln_b, alpha):
    n_exp = raw["moe_w_router"].shape[-1]
    routes = [_router(s["x"], raw["moe_w_router"][layer], raw["moe_b_router"][layer]) for s in streams]
    route_all = jnp.concatenate(routes, axis=0)
    x_all = jnp.concatenate([s["x"] for s in streams], axis=0)
    bm = MOE_ROW_TILE
    meta = _route_meta(route_all[:, 2:4].astype(jnp.int32), n_exp, bm)
    xs = _moe_gather(meta["row_token"], x_all)
    h = _moe_swiglu(meta, xs, raw["moe_w_gu"], layer, bm)
    y = _moe_down(meta, h, w["moe_w_down"][layer], bm)
    off = 0
    for s, route in zip(streams, routes):
        pos = meta["pos"][2 * off:2 * (off + s["m"])]
        s["x"], s["xb"] = _moe_combine(pos, y, route, s["x"], ln_g, ln_b, alpha)
        off += s["m"]


def _ple_body(xb_ref, x_ref, p_ref, wp_ref, wg_ref, o_ref, ob_ref):
    gate = _sigmoid(_dot(xb_ref[...], wg_ref[...]))
    y = x_ref[...] + _dot(p_ref[...], wp_ref[...]) * gate
    o_ref[...] = y
    ob_ref[...] = y.astype(ob_ref.dtype)


def _ple(x, xb, pb, w_proj, w_gate):
    m, d = x.shape
    dp = pb.shape[-1]
    bm = _tile(m, 1024, SUBLANES * 2)
    bn = _tile(d, 512, LANES)
    return pl.pallas_call(
        _ple_body,
        grid=(m // bm, d // bn),
        in_specs=[
            pl.BlockSpec((bm, d), lambda i, j: (i, 0)),
            pl.BlockSpec((bm, bn), lambda i, j: (i, j)),
            pl.BlockSpec((bm, dp), lambda i, j: (i, 0)),
            pl.BlockSpec((dp, bn), lambda i, j: (0, j)),
            pl.BlockSpec((d, bn), lambda i, j: (0, j)),
        ],
        out_specs=[
            pl.BlockSpec((bm, bn), lambda i, j: (i, j)),
            pl.BlockSpec((bm, bn), lambda i, j: (i, j)),
        ],
        out_shape=[jax.ShapeDtypeStruct((m, d), F32), jax.ShapeDtypeStruct((m, d), BF16)],
        compiler_params=_params("parallel", "arbitrary"),
        name="ple",
    )(xb, x, pb, w_proj, w_gate)


def _heads_body(x_ref, w_ref, cos_ref, sin_ref, o_ref, or_ref, *, heads_per_blk, rope_blocks, two_out):
    acc = _dot(x_ref[...], w_ref[...])
    cos = cos_ref[...]
    sin = sin_ref[...]

    def roped():
        return [_rope_head(acc[:, h * LANES:(h + 1) * LANES], cos, sin) for h in range(heads_per_blk)]

    if two_out:
        o_ref[...] = acc.astype(o_ref.dtype)
        for h, y in enumerate(roped()):
            or_ref[:, h * LANES:(h + 1) * LANES] = y.astype(or_ref.dtype)
    else:
        j = pl.program_id(1)
        is_rope = functools.reduce(jnp.logical_or, [j == rb for rb in rope_blocks])

        @pl.when(is_rope)
        def _():
            for h, y in enumerate(roped()):
                o_ref[:, h * LANES:(h + 1) * LANES] = y
                or_ref[:, h * LANES:(h + 1) * LANES] = y.astype(or_ref.dtype)

        @pl.when(jnp.logical_not(is_rope))
        def _():
            o_ref[...] = acc
            or_ref[...] = acc.astype(or_ref.dtype)


def _heads_proj(xb, w, cos, sin, *, two_out, rope_blocks, out_dtypes):
    m, d = xb.shape
    n = w.shape[-1]
    bn = 4 * LANES
    bm = _tile(m, 1024, SUBLANES * 2)
    return pl.pallas_call(
        functools.partial(_heads_body, heads_per_blk=bn // LANES, rope_blocks=rope_blocks, two_out=two_out),
        grid=(m // bm, n // bn),
        in_specs=[
            pl.BlockSpec((bm, d), lambda i, j: (i, 0)),
            pl.BlockSpec((d, bn), lambda i, j: (0, j)),
            pl.BlockSpec((bm, LANES), lambda i, j: (i, 0)),
            pl.BlockSpec((bm, LANES), lambda i, j: (i, 0)),
        ],
        out_specs=[
            pl.BlockSpec((bm, bn), lambda i, j: (i, j)),
            pl.BlockSpec((bm, bn), lambda i, j: (i, j)),
        ],
        out_shape=[jax.ShapeDtypeStruct((m, n), out_dtypes[0]), jax.ShapeDtypeStruct((m, n), out_dtypes[1])],
        compiler_params=_params("parallel", "arbitrary"),
        name="heads_proj",
    )(xb, w, cos, sin)


def _gates_body(x_ref, w_ref, o_ref):
    o_ref[...] = _sigmoid(_dot(x_ref[...], w_ref[...]))


def _gates_proj(xb, w):
    m, d = xb.shape
    n = w.shape[-1]
    bm = _tile(m, 1024, SUBLANES * 2)
    return pl.pallas_call(
        _gates_body,
        grid=(m // bm,),
        in_specs=[pl.BlockSpec((bm, d), lambda i: (i, 0)), pl.BlockSpec((d, n), lambda i: (0, 0))],
        out_specs=pl.BlockSpec((bm, n), lambda i: (i, 0)),
        out_shape=jax.ShapeDtypeStruct((m, n), F32),
        compiler_params=_params("parallel"),
        name="gates_proj",
    )(xb, w)


def _gather_body(pt_ref, x_ref, oc_ref, os_ref):
    half = oc_ref.shape[-1]
    oc_ref[...] = x_ref[:, :half]
    os_ref[...] = x_ref[:, half:].astype(os_ref.dtype)


def _gather_pages(cache2, page_table):
    n_pool, page, width = cache2.shape
    nb, n_pages = page_table.shape
    half = width // 2
    grid_spec = pltpu.PrefetchScalarGridSpec(
        num_scalar_prefetch=1,
        grid=(nb, n_pages),
        in_specs=[pl.BlockSpec((None, page, width), lambda b, p, pt: (pt[b, p], 0, 0))],
        out_specs=[
            pl.BlockSpec((None, page, half), lambda b, p, pt: (b, p, 0)),
            pl.BlockSpec((None, page, half), lambda b, p, pt: (b, p, 0)),
        ],
    )
    return pl.pallas_call(
        _gather_body,
        grid_spec=grid_spec,
        out_shape=[
            jax.ShapeDtypeStruct((nb, n_pages * page, half), F32),
            jax.ShapeDtypeStruct((nb, n_pages * page, half), BF16),
        ],
        compiler_params=_params("parallel", "arbitrary"),
        name="gather_pages",
    )(page_table, cache2)


def _compress_body(x_ref, pe_ref, w_ref, o_ref, *, nc):
    half = nc // 2
    acc = jnp.zeros((nc, LANES), F32)
    for l in range(CMP_BLOCK):
        xe = x_ref[pl.ds(l, half, stride=2 * CMP_BLOCK), :]
        xo = x_ref[pl.ds(l + CMP_BLOCK, half, stride=2 * CMP_BLOCK), :]
        xl = jnp.concatenate([xe, xo], axis=0) + pe_ref[l:l + 1, :]
        acc = acc + _dot(xl.astype(BF16), w_ref[l])
    o_ref[...] = acc.astype(o_ref.dtype)


def _compress(x3, pe, w, n_kv, nc):
    nb = x3.shape[0]
    tc = nc * CMP_BLOCK
    return pl.pallas_call(
        functools.partial(_compress_body, nc=nc),
        grid=(nb, 2, n_kv),
        in_specs=[
            pl.BlockSpec((None, tc, LANES), lambda b, c, g: (b, 0, c * n_kv + g)),
            pl.BlockSpec((None, CMP_BLOCK, LANES), lambda b, c, g: (c, 0, 0)),
            pl.BlockSpec((None, CMP_BLOCK, LANES, LANES), lambda b, c, g: (c, 0, 0, 0)),
        ],
        out_specs=pl.BlockSpec((None, None, None, nc, LANES), lambda b, c, g: (c, b, g, 0, 0)),
        out_shape=jax.ShapeDtypeStruct((2, nb, n_kv, nc, LANES), BF16),
        compiler_params=_params("parallel", "parallel", "arbitrary"),
        name="compress",
    )(x3, pe, w)


def _cmp_block_of_lane(nc, shape):
    lane = lax.broadcasted_iota(jnp.int32, shape, len(shape) - 1)
    half = nc // 2
    return jnp.where(lane < half, 2 * lane, 2 * (lane - half) + 1)


def _cmp_branch(q, kc, vc, trow, nc, scale):
    s = _dot_nt(q, kc) * scale
    n = _cmp_block_of_lane(nc, s.shape)
    ok = ((n + 1) * CMP_BLOCK - 1) <= trow
    sm = jnp.where(ok, s, NEG)
    e = jnp.exp(sm - jnp.max(sm, axis=-1, keepdims=True))
    p = jnp.where(ok, e / jnp.sum(e, axis=-1, keepdims=True), 0.0)
    return p, _dot(p.astype(BF16), vc)


def _online_update(carry, s, ok, v):
    m, l, acc = carry
    sm = jnp.where(ok, s, NEG)
    m_new = jnp.maximum(m, jnp.max(sm, axis=-1, keepdims=True))
    a = jnp.exp(m - m_new)
    p = jnp.where(ok, jnp.exp(sm - m_new), 0.0)
    return m_new, a * l + jnp.sum(p, axis=-1, keepdims=True), a * acc + _dot(p.astype(BF16), v)


def _online_init(rows):
    return (jnp.full((rows, 1), NEG, F32), jnp.zeros((rows, 1), F32), jnp.zeros((rows, LANES), F32))


def _block_mask(sel_rows, k0, tk):
    nbk = sel_rows.shape[-1]
    blk = lax.broadcasted_iota(jnp.int32, (nbk, tk), 0)
    kpos = k0 + lax.broadcasted_iota(jnp.int32, (nbk, tk), 1)
    expand = (blk == kpos // SEL_BLOCK).astype(BF16)
    return _dot(sel_rows, expand)


def _attn_prompt_body(q_ref, qr_ref, gt_ref, kc_ref, vc_ref, ks_ref, vs_ref, kw_ref, vw_ref, o_ref,
                      *, n_rep, tq, seq, nc, nsb, scale):
    t0 = pl.program_id(2) * tq
    rows = n_rep * tq
    q = jnp.concatenate([q_ref[:, r * LANES:(r + 1) * LANES] for r in range(n_rep)], axis=0)
    qr = jnp.concatenate([qr_ref[:, r * LANES:(r + 1) * LANES] for r in range(n_rep)], axis=0)
    trow = t0 + lax.rem(lax.broadcasted_iota(jnp.int32, (rows, 1), 0), tq)

    p_cmp, o_cmp = _cmp_branch(q, kc_ref[...], vc_ref[...], trow, nc, scale)
    imp = p_cmp[0:tq]
    for r in range(1, n_rep):
        imp = imp + p_cmp[r * tq:(r + 1) * tq]
    half = nc // 2
    imp = imp[:, :half] + imp[:, half:nc]
    imp = jnp.concatenate([imp, jnp.zeros((tq, LANES - half), F32)], axis=1)
    imp_t = imp.T[:nsb]

    j = lax.broadcasted_iota(jnp.int32, (nsb, tq), 0)
    tcol = t0 + lax.broadcasted_iota(jnp.int32, (nsb, tq), 1)
    cur = tcol // SEL_BLOCK
    forced = (j == 0) | (j == cur) | (j == cur - 1)
    avail = j * SEL_BLOCK <= tcol
    work = jnp.where(avail, imp_t + FORCE * forced.astype(F32), -1.0)
    sel = jnp.zeros((nsb, tq), F32)
    for _ in range(min(N_SEL, nsb)):
        mx = jnp.max(work, axis=0, keepdims=True)
        idx = jnp.min(jnp.where(work == mx, j, nsb), axis=0, keepdims=True)
        pick = j == idx
        sel = jnp.where(pick, 1.0, sel)
        work = jnp.where(pick, -jnp.inf, work)
    sel = jnp.where(avail, sel, 0.0)
    sel = jnp.concatenate([sel, jnp.zeros((LANES - nsb, tq), F32)], axis=0).T
    sel_rows = jnp.concatenate([sel.astype(BF16)] * n_rep, axis=0)

    tk = min(512, seq)
    n_tiles = (t0 + tq + tk - 1) // tk

    def sel_step(kt, carry):
        k0 = pl.multiple_of(kt * tk, tk)
        s = _dot_nt(qr, ks_ref[pl.ds(k0, tk), :]) * scale
        kpos = k0 + lax.broadcasted_iota(jnp.int32, (rows, tk), 1)
        ok = (_block_mask(sel_rows, k0, tk) > 0.5) & (kpos <= trow)
        return _online_update(carry, s, ok, vs_ref[pl.ds(k0, tk), :])

    _, l_sel, acc_sel = lax.fori_loop(0, n_tiles, sel_step, _online_init(rows))
    o_sel = acc_sel / l_sel

    wl = min(WINDOW + tq, seq)
    w0 = pl.multiple_of(jnp.maximum(t0 + tq - wl, 0), tq)
    s = _dot_nt(qr, kw_ref[pl.ds(w0, wl), :]) * scale
    dist = trow - (w0 + lax.broadcasted_iota(jnp.int32, (rows, wl), 1))
    ok = (dist >= 0) & (dist < WINDOW)
    _, l_win, acc_win = _online_update(_online_init(rows), s, ok, vw_ref[pl.ds(w0, wl), :])
    o_win = acc_win / l_win

    for r in range(n_rep):
        sl = slice(r * tq, (r + 1) * tq)
        o = (gt_ref[:, r:r + 1] * o_cmp[sl]
             + gt_ref[:, n_rep + r:n_rep + r + 1] * o_sel[sl]
             + gt_ref[:, 2 * n_rep + r:2 * n_rep + r + 1] * o_win[sl])
        o_ref[:, r * LANES:(r + 1) * LANES] = o.astype(o_ref.dtype)


def _attn_prompt(q, qr, gates_g, cmp_kv, kvb, nb, seq, n_kv):
    m, d = q.shape
    n_rep = d // (n_kv * LANES)
    nc = seq // CMP_BLOCK
    nsb = -(-seq // SEL_BLOCK)
    tq = _tile(seq, 128, LANES)
    nq = seq // tq
    hw = n_rep * LANES
    kv_spec = lambda part: pl.BlockSpec((seq, LANES), lambda b, g, i: (b, part * n_kv + g))
    cmp_spec = lambda c: pl.BlockSpec((None, None, None, nc, LANES), lambda b, g, i: (c, b, g, 0, 0))
    return pl.pallas_call(
        functools.partial(_attn_prompt_body, n_rep=n_rep, tq=tq, seq=seq, nc=nc, nsb=nsb,
                          scale=float(LANES) ** -0.5),
        grid=(nb, n_kv, nq),
        in_specs=[
            pl.BlockSpec((tq, hw), lambda b, g, i: (b * nq + i, g)),
            pl.BlockSpec((tq, hw), lambda b, g, i: (b * nq + i, g)),
            pl.BlockSpec((None, tq, 3 * n_rep), lambda b, g, i: (g, b * nq + i, 0)),
            cmp_spec(0), cmp_spec(1),
            kv_spec(2), kv_spec(3), kv_spec(4), kv_spec(5),
        ],
        out_specs=pl.BlockSpec((tq, hw), lambda b, g, i: (b * nq + i, g)),
        out_shape=jax.ShapeDtypeStruct((m, d), BF16),
        compiler_params=_params("parallel", "parallel", "arbitrary"),
        name="attn_prompt",
    )(q, qr, gates_g, cmp_kv, cmp_kv, kvb, kvb, kvb, kvb)


def _pad_rows(a, rows):
    return jnp.concatenate([a, jnp.zeros((rows - a.shape[0], a.shape[1]), a.dtype)], axis=0)


def _attn_sample_body(q_ref, qr_ref, gt_ref, kc_ref, vc_ref, ksp_ref, vsp_ref, ksn_ref, vsn_ref,
                      kwp_ref, vwp_ref, kwn_ref, vwn_ref, o_ref,
                      *, n_rep, tn, past, nc, nsb, nbk, scale):
    rows = n_rep * tn
    q = jnp.concatenate([q_ref[:, r * LANES:(r + 1) * LANES] for r in range(n_rep)], axis=0).astype(BF16)
    qr = jnp.concatenate([qr_ref[:, r * LANES:(r + 1) * LANES] for r in range(n_rep)], axis=0).astype(BF16)
    trow = past + lax.rem(lax.broadcasted_iota(jnp.int32, (rows, 1), 0), tn)

    p_cmp, o_cmp = _cmp_branch(q, kc_ref[...], vc_ref[...], trow, nc, scale)
    imp = p_cmp[0:tn]
    for r in range(1, n_rep):
        imp = imp + p_cmp[r * tn:(r + 1) * tn]
    half = nc // 2
    imp = imp[:, :half] + imp[:, half:nc]
    imp = jnp.concatenate([imp, jnp.zeros((tn, nbk - half), F32)], axis=1)

    j = lax.broadcasted_iota(jnp.int32, (tn, nbk), 1)
    tcol = past + lax.broadcasted_iota(jnp.int32, (tn, nbk), 0)
    cur = tcol // SEL_BLOCK
    forced = (j == 0) | (j == cur) | (j == cur - 1)
    avail = (j * SEL_BLOCK <= tcol) & (j < nsb)
    work = jnp.where(avail, imp + FORCE * forced.astype(F32), -1.0)
    work = jnp.where(j < nsb, work, -jnp.inf)
    sel = jnp.zeros((tn, nbk), F32)
    for _ in range(min(N_SEL, nsb)):
        mx = jnp.max(work, axis=1, keepdims=True)
        idx = jnp.min(jnp.where(work == mx, j, nbk), axis=1, keepdims=True)
        pick = j == idx
        sel = jnp.where(pick, 1.0, sel)
        work = jnp.where(pick, -jnp.inf, work)
    sel = jnp.where(avail, sel, 0.0)
    sel_rows = jnp.concatenate([sel.astype(BF16)] * n_rep, axis=0)

    tk = min(1024, past)

    def sel_step(kt, carry):
        k0 = pl.multiple_of(kt * tk, tk)
        s = _dot_nt(qr, ksp_ref[pl.ds(k0, tk), :]) * scale
        kpos = k0 + lax.broadcasted_iota(jnp.int32, (rows, tk), 1)
        ok = (_block_mask(sel_rows, k0, tk) > 0.5) & (kpos <= trow)
        return _online_update(carry, s, ok, vsp_ref[pl.ds(k0, tk), :])

    carry = lax.fori_loop(0, past // tk, sel_step, _online_init(rows))
    kn = _pad_rows(ksn_ref[...], LANES).astype(BF16)
    vn = _pad_rows(vsn_ref[...], LANES).astype(BF16)
    kpos = past + lax.broadcasted_iota(jnp.int32, (rows, LANES), 1)
    ok = (_block_mask(sel_rows, past, LANES) > 0.5) & (kpos <= trow)
    _, l_sel, acc_sel = _online_update(carry, _dot_nt(qr, kn) * scale, ok, vn)
    o_sel = acc_sel / l_sel

    wb = kwp_ref.shape[0]
    dist = trow - (past - wb + lax.broadcasted_iota(jnp.int32, (rows, wb), 1))
    ok = (dist >= 0) & (dist < WINDOW)
    carry = _online_update(_online_init(rows), _dot_nt(qr, kwp_ref[...].astype(BF16)) * scale, ok,
                           vwp_ref[...].astype(BF16))
    kn = _pad_rows(kwn_ref[...], LANES).astype(BF16)
    vn = _pad_rows(vwn_ref[...], LANES).astype(BF16)
    dist = trow - kpos
    ok = (dist >= 0) & (dist < WINDOW)
    _, l_win, acc_win = _online_update(carry, _dot_nt(qr, kn) * scale, ok, vn)
    o_win = acc_win / l_win

    for r in range(n_rep):
        sl = slice(r * tn, (r + 1) * tn)
        o = (gt_ref[:, r:r + 1] * o_cmp[sl]
             + gt_ref[:, n_rep + r:n_rep + r + 1] * o_sel[sl]
             + gt_ref[:, 2 * n_rep + r:2 * n_rep + r + 1] * o_win[sl])
        o_ref[:, r * LANES:(r + 1) * LANES] = o


def _attn_sample(q, qr, gates_g, cmp_kv, sel_past, kv_new, win_past, nb, tn, n_kv):
    m, d = q.shape
    n_rep = d // (n_kv * LANES)
    past = sel_past.shape[1]
    nc = cmp_kv.shape[3]
    nsb = -(-(past + tn) // SEL_BLOCK)
    nbk = -(-nsb // LANES) * LANES
    wb = win_past.shape[1]
    hw = n_rep * LANES
    new_spec = lambda part: pl.BlockSpec((tn, LANES), lambda b, g: (b, part * n_kv + g))
    cmp_spec = lambda c: pl.BlockSpec((None, None, None, nc, LANES), lambda b, g: (c, b, g, 0, 0))
    return pl.pallas_call(
        functools.partial(_attn_sample_body, n_rep=n_rep, tn=tn, past=past, nc=nc, nsb=nsb, nbk=nbk,
                          scale=float(LANES) ** -0.5),
        grid=(nb, n_kv),
        in_specs=[
            pl.BlockSpec((tn, hw), lambda b, g: (b, g)),
            pl.BlockSpec((tn, hw), lambda b, g: (b, g)),
            pl.BlockSpec((None, tn, 3 * n_rep), lambda b, g: (g, b, 0)),
            cmp_spec(0), cmp_spec(1),
            pl.BlockSpec((None, past, LANES), lambda b, g: (b, 0, g)),
            pl.BlockSpec((None, past, LANES), lambda b, g: (b, 0, n_kv + g)),
            new_spec(2), new_spec(3),
            pl.BlockSpec((None, wb, LANES), lambda b, g: (b, 0, g)),
            pl.BlockSpec((None, wb, LANES), lambda b, g: (b, 0, n_kv + g)),
            new_spec(4), new_spec(5),
        ],
        out_specs=pl.BlockSpec((tn, hw), lambda b, g: (b, g)),
        out_shape=jax.ShapeDtypeStruct((m, d), F32),
        compiler_params=_params("parallel", "arbitrary"),
        name="attn_sample",
    )(q, qr, gates_g, cmp_kv, cmp_kv, sel_past, sel_past, kv_new, kv_new, win_past, win_past, kv_new, kv_new)


def _rope_tables(pos):
    half = LANES // 2
    inv = ROPE_THETA ** (-jnp.arange(half, dtype=F32) / half)
    ang = pos.astype(F32)[:, None] * inv[None, :]
    cos, sin = jnp.cos(ang), jnp.sin(ang)
    return jnp.concatenate([cos, cos], axis=-1), jnp.concatenate([-sin, sin], axis=-1)


def _prep_weights(a_w_in, a_w_out, w_kv, cmp_pe, cmp_w, b_w_qg, b_w_o, ffn_w_gu, ffn_w_down,
                  moe_w_down, ple_w_proj, ple_w_gate, n_kv):
    d = a_w_in.shape[1]
    n_heads = d // LANES
    n_rep = n_heads // n_kv
    f = ffn_w_down.shape[1]
    fp = -(-f // 512) * 512
    padf = ((0, 0), (0, 0), (0, fp - f))
    w = {}
    w["a_w_in"] = a_w_in.astype(BF16)
    w["a_w_out"] = a_w_out.astype(BF16)
    w["w_kv"] = w_kv.astype(BF16)
    w["cmp_pe"] = jnp.transpose(cmp_pe, (1, 0, 2))
    w["cmp_w"] = jnp.transpose(cmp_w, (1, 0, 2, 3)).astype(BF16)
    w["w_q"] = b_w_qg[:, :, :d].astype(BF16)
    wg = b_w_qg[:, :, d:].reshape(-1, d, n_kv, n_rep, 3)
    wg = jnp.transpose(wg, (0, 1, 2, 4, 3)).reshape(-1, d, 3 * n_heads)
    w["w_gates"] = jnp.pad(wg, ((0, 0), (0, 0), (0, LANES - 3 * n_heads))).astype(BF16)
    w["b_w_o"] = b_w_o.astype(BF16)
    w["ffn_w_gu"] = jnp.concatenate([jnp.pad(ffn_w_gu[:, :, :f], padf), jnp.pad(ffn_w_gu[:, :, f:], padf)],
                                    axis=-1).astype(BF16)
    w["ffn_w_down"] = jnp.pad(ffn_w_down, ((0, 0), (0, fp - f), (0, 0))).astype(BF16)
    w["ffn_f"] = fp
    w["moe_w_down"] = moe_w_down.astype(BF16)
    w["ple_w_proj"] = ple_w_proj.astype(BF16)
    w["ple_w_gate"] = ple_w_gate.astype(BF16)
    return w


def _new_stream(x3, p4, conv_prev, past_ctx):
    nb, seq, d = x3.shape
    m = nb * seq
    past = 0 if past_ctx is None else past_ctx["past"]
    cos1, sin1 = _rope_tables(past + jnp.arange(seq, dtype=jnp.int32))
    x = x3.reshape(m, d)
    return dict(nb=nb, seq=seq, m=m, x=x, xb=x.astype(BF16), pb=p4.reshape(p4.shape[0], m, -1).astype(BF16),
                conv_prev=conv_prev, past_ctx=past_ctx, cos=jnp.tile(cos1, (nb, 1)), sin=jnp.tile(sin1, (nb, 1)),
                conv_states=[], ctx=None)


def _conv_mixer(s, i, w, raw, ln_g, ln_b, alpha):
    d = s["x"].shape[1]
    prev = jnp.zeros((s["nb"], CONV_W - 1, d), F32) if s["conv_prev"] is None else s["conv_prev"][i]
    v, st = _conv_in(s["xb"], w["a_w_in"][i], raw["a_conv_w"][i], prev, s["seq"])
    s["conv_states"].append(st[:, SUBLANES - (CONV_W - 1):])
    s["x"], s["xb"] = _proj_ln(v[None], w["a_w_out"][i][None], s["x"], ln_g, ln_b, alpha)


def _nsa_mixer(s, li, w, ln_g, ln_b, alpha, n_kv):
    m, d = s["x"].shape
    n_heads = d // LANES
    n_rep = n_heads // n_kv
    prompt = s["past_ctx"] is None
    qdt = BF16 if prompt else F32
    q, qr = _heads_proj(s["xb"], w["w_q"][li], s["cos"], s["sin"], two_out=True, rope_blocks=(),
                        out_dtypes=(qdt, qdt))
    gates = _gates_proj(s["xb"], w["w_gates"][li])[:, :3 * n_heads]
    gates_g = jnp.transpose(gates.reshape(m, n_kv, 3 * n_rep), (1, 0, 2))
    ctx = s["ctx"]
    if prompt:
        o = _attn_prompt(q, qr, gates_g, ctx["cmp_kv"], ctx["kvb"], s["nb"], s["seq"], n_kv)
    else:
        o = _attn_sample(q, qr, gates_g, ctx["cmp_kv"], s["past_ctx"]["sel_past"], ctx["kv_all"],
                         s["past_ctx"]["win_past"], s["nb"], s["seq"], n_kv).astype(BF16)
    s["x"], s["xb"] = _proj_ln(o[None], w["b_w_o"][li][None], s["x"], ln_g, ln_b, alpha)


def _shared_context(s, w, n_kv):
    kv_all, kvb = _heads_proj(s["xb"], w["w_kv"], s["cos"], s["sin"], two_out=False, rope_blocks=(2, 4),
                              out_dtypes=(F32, BF16))
    if s["past_ctx"] is None:
        cmp_kv = _compress(kv_all.reshape(s["nb"], s["seq"], -1), w["cmp_pe"], w["cmp_w"], n_kv,
                           s["seq"] // CMP_BLOCK)
    else:
        past = s["past_ctx"]["past"]
        assert past % CMP_BLOCK == 0 and s["seq"] < CMP_BLOCK
        cmp_kv = _compress(s["past_ctx"]["cmp_past"], w["cmp_pe"], w["cmp_w"], n_kv, past // CMP_BLOCK)
    s["ctx"] = dict(kv_all=kv_all, kvb=kvb, cmp_kv=cmp_kv)


def _trunk(streams, w, raw, n_kv):
    depth = raw["ln_g"].shape[0]
    n_a = depth // 2
    alpha = float((2 * depth) ** 0.25)
    for i in range(depth):
        ln_g, ln_b = raw["ln_g"][i], raw["ln_b"][i]
        for s in streams:
            if i < n_a:
                _conv_mixer(s, i, w, raw, ln_g[0], ln_b[0], alpha)
            else:
                _nsa_mixer(s, i - n_a, w, ln_g[0], ln_b[0], alpha, n_kv)
        if i % 2 == 0:
            for s in streams:
                h = _swiglu(s["xb"], w["ffn_w_gu"][i // 2], w["ffn_f"])
                s["x"], s["xb"] = _proj_ln(h[None], w["ffn_w_down"][i // 2][None], s["x"], ln_g[1], ln_b[1], alpha)
        else:
            _moe_layer(streams, i // 2, raw, w, ln_g[1], ln_b[1], alpha)
        for s in streams:
            s["x"], s["xb"] = _ple(s["x"], s["xb"], s["pb"][i], w["ple_w_proj"][i], w["ple_w_gate"][i])
            if i == n_a - 1:
                _shared_context(s, w, n_kv)
    outs = []
    for s in streams:
        nb, seq, hd_blk = s["nb"], s["seq"], n_kv * LANES
        kv_all = s["ctx"]["kv_all"]
        rows = kv_all[:, :4 * hd_blk].reshape(nb, seq, 4, n_kv, LANES)
        win = kv_all[:, 4 * hd_blk:].reshape(nb, seq, 2, n_kv, LANES)
        outs.append((s["x"].reshape(nb, seq, -1), rows, win, jnp.stack(s["conv_states"])))
    return outs


def kernel(x_prompt, x_sample, cache_kv, cache_win, state_conv, page_table, p_prompt, p_sample,
           a_w_in, a_conv_w, a_w_out, w_kv, cmp_pe, cmp_w, b_w_qg, b_w_o, ffn_w_gu, ffn_w_down,
           moe_w_router, moe_b_router, moe_w_gu, moe_w_down, ple_w_proj, ple_w_gate, ln_g, ln_b):
    n_kv = cache_kv.shape[3]
    raw = dict(a_conv_w=a_conv_w, moe_w_router=moe_w_router, moe_b_router=moe_b_router, moe_w_gu=moe_w_gu,
               ln_g=ln_g, ln_b=ln_b)
    w = _prep_weights(a_w_in, a_w_out, w_kv, cmp_pe, cmp_w, b_w_qg, b_w_o, ffn_w_gu, ffn_w_down,
                      moe_w_down, ple_w_proj, ple_w_gate, n_kv)

    n_pool, page = cache_kv.shape[:2]
    cmp_past, sel_past = _gather_pages(cache_kv.reshape(n_pool, page, -1), page_table)
    nb_s, wb = cache_win.shape[:2]
    past_ctx = dict(past=page_table.shape[1] * page, cmp_past=cmp_past, sel_past=sel_past,
                    win_past=cache_win.reshape(nb_s, wb, -1))

    streams = [_new_stream(x_prompt, p_prompt, None, None),
               _new_stream(x_sample, p_sample, state_conv, past_ctx)]
    (y_p, rows_p, win_p, conv_p), (y_s, rows_s, win_s, conv_s) = _trunk(streams, w, raw, n_kv)
    win_state_p = win_p[:, -min(WINDOW, x_prompt.shape[1]):]
    win_state_s = jnp.concatenate([cache_win.astype(win_s.dtype), win_s], axis=1)[:, -wb:]

    return (y_p, y_s, rows_p, rows_s, win_state_p, win_state_s, conv_p, conv_s)
```

```python
import functools

import jax
import jax.numpy as jnp
from jax import lax
from jax.experimental import pallas as pl
from jax.experimental.pallas import tpu as pltpu

F32 = jnp.float32
BF16 = jnp.bfloat16

CONV_W = 3
CMP_BLOCK = 32
SEL_BLOCK = 64
N_SEL = 16
WINDOW = 512
ROPE_THETA = 10000.0
LN_EPS = 1e-5
NEG = -1e30
FORCE = 1e4

LANES = 128
SUBLANES = 8
VMEM_LIMIT_BYTES = 56 * 1024 * 1024


def _params(*sem):
    return pltpu.CompilerParams(dimension_semantics=sem, vmem_limit_bytes=VMEM_LIMIT_BYTES)


def _tile(n, pref, align):
    if n <= pref:
        return n
    t = (pref // align) * align
    while t >= align:
        if n % t == 0:
            return t
        t -= align
    raise ValueError(f"no tile for {n} (pref {pref}, align {align})")


def _dot(a, b):
    return jnp.dot(a, b, preferred_element_type=F32)


def _dot_nt(a, b):
    return lax.dot_general(a, b, (((1,), (1,)), ((), ())), preferred_element_type=F32)


def _sigmoid(x):
    return 1.0 / (1.0 + jnp.exp(-x))


def _rope_head(a, cos, sin_signed):
    return a * cos + pltpu.roll(a, a.shape[-1] // 2, axis=1) * sin_signed


def _layer_norm(z, g, b):
    mu = jnp.mean(z, axis=-1, keepdims=True)
    zc = z - mu
    var = jnp.mean(zc * zc, axis=-1, keepdims=True)
    return zc * lax.rsqrt(var + LN_EPS) * g + b


WS_STAGE_BYTES = 4 * 1024 * 1024


def _ws_body(te_ref, nu_ref, x_ref, w_hbm, *rest, layer, seg_offsets, bn, ck, nj_valid, epilogue, n_extra, n_out):
    extra = rest[:n_extra]
    outs = rest[n_extra:n_extra + n_out]
    cache, stage, acc_ref, sem = rest[n_extra + n_out:]
    j = pl.program_id(0)
    i = pl.program_id(1)
    e = te_ref[i]
    nseg = len(seg_offsets)
    n_chunks = x_ref.shape[1] // ck
    col_ok = j < nj_valid
    valid = jnp.logical_and(i < nu_ref[0], col_ok)
    changed = jnp.logical_and(jnp.logical_or(i == 0, e != te_ref[jnp.maximum(i - 1, 0)]), col_ok)

    def copies(c, slot):
        return [pltpu.make_async_copy(
            w_hbm.at[layer, e, pl.ds(c * ck, ck), pl.ds(pl.multiple_of(off + j * bn, LANES), bn)],
            stage.at[slot, s], sem.at[slot]) for s, off in enumerate(seg_offsets)]

    @pl.when(changed)
    def _():
        for cp in copies(0, 0):
            cp.start()
        for c in range(n_chunks):
            slot = c % 2
            if c + 1 < n_chunks:
                for cp in copies(c + 1, 1 - slot):
                    cp.start()
            for cp in copies(c, slot):
                cp.wait()
            for s in range(nseg):
                cache[c * ck:(c + 1) * ck, s * bn:(s + 1) * bn] = stage[slot, s].astype(BF16)
            part = _dot(x_ref[:, c * ck:(c + 1) * ck], cache[c * ck:(c + 1) * ck, :])
            if c == 0:
                acc_ref[...] = part
            else:
                acc_ref[...] += part

    @pl.when(jnp.logical_and(valid, jnp.logical_not(changed)))
    def _():
        acc_ref[...] = _dot(x_ref[...], cache[...])

    @pl.when(valid)
    def _():
        epilogue(acc_ref, extra, outs)

    @pl.when(jnp.logical_not(valid))
    def _():
        for o in outs:
            o[...] = jnp.zeros_like(o)


def _ws_matmul(x, w4, layer, seg_offsets, bn, nj, nj_valid, bm, epilogue, extra, extra_specs, out_shape, out_specs,
               name, meta=None):
    rows, k = x.shape
    n_tiles = rows // bm
    nseg = len(seg_offsets)
    ck = _tile(k, max(LANES, WS_STAGE_BYTES // (2 * nseg * bn * 4)), LANES)
    if meta is None:
        te = jnp.zeros((n_tiles,), jnp.int32)
        nu = jnp.full((1,), n_tiles, jnp.int32)
    else:
        te, nu = meta["tile_expert"], meta["n_used"]
    grid_spec = pltpu.PrefetchScalarGridSpec(
        num_scalar_prefetch=2,
        grid=(nj, n_tiles),
        in_specs=[pl.BlockSpec((bm, k), lambda j, i, te, nu: (i, 0)), pl.BlockSpec(memory_space=pl.ANY)]
        + list(extra_specs),
        out_specs=out_specs,
        scratch_shapes=[pltpu.VMEM((k, nseg * bn), BF16), pltpu.VMEM((2, nseg, ck, bn), F32),
                        pltpu.VMEM((bm, nseg * bn), F32), pltpu.SemaphoreType.DMA((2,))],
    )
    return pl.pallas_call(
        functools.partial(_ws_body, layer=layer, seg_offsets=tuple(seg_offsets), bn=bn, ck=ck, nj_valid=nj_valid,
                          epilogue=epilogue, n_extra=len(extra), n_out=len(out_specs)),
        grid_spec=grid_spec,
        out_shape=out_shape,
        compiler_params=_params("arbitrary", "arbitrary"),
        name=name,
    )(te, nu, x, w4, *extra)


def _epi_swiglu(acc_ref, extra, outs):
    bn = outs[0].shape[-1]
    g = acc_ref[:, :bn]
    outs[0][...] = (g * _sigmoid(g) * acc_ref[:, bn:]).astype(outs[0].dtype)


def _epi_copy(acc_ref, extra, outs):
    outs[0][...] = acc_ref[...]


def _epi_ln(acc_ref, extra, outs, *, alpha, br):
    res_ref, g_ref, b_ref = extra
    o_ref, ob_ref = outs

    def rows(r, carry):
        sl = pl.ds(pl.multiple_of(r * br, br), br)
        y = _layer_norm(alpha * res_ref[sl, :] + acc_ref[sl, :], g_ref[...], b_ref[...])
        o_ref[sl, :] = y
        ob_ref[sl, :] = y.astype(ob_ref.dtype)
        return carry

    lax.fori_loop(0, o_ref.shape[0] // br, rows, 0)


def _swiglu_ws(x, w_gu4, layer, bm, meta=None, f_pad=None):
    rows = x.shape[0]
    f = w_gu4.shape[-1] // 2
    f_out = f if f_pad is None else f_pad
    bn = _tile(f, 1024, LANES) if f % 1024 == 0 else _tile(f, 256, LANES)
    return _ws_matmul(
        x, w_gu4, layer, (0, f), bn, f_out // bn, f // bn, bm, _epi_swiglu, (), (),
        [jax.ShapeDtypeStruct((rows, f_out), BF16)],
        [pl.BlockSpec((bm, bn), lambda j, i, te, nu: (i, j))], "swiglu_ws", meta)[0]


def _proj_ln_ws(lhs, w4, layer, res, g, b, alpha):
    m, d = res.shape
    bm = _tile(m, 128, SUBLANES * 2)
    br = _tile(bm, 32, SUBLANES * 2)
    row_spec = pl.BlockSpec((bm, d), lambda j, i, te, nu: (i, 0))
    vec_spec = pl.BlockSpec((1, d), lambda j, i, te, nu: (0, 0))
    return _ws_matmul(
        lhs, w4, layer, (0,), d, 1, 1, bm, functools.partial(_epi_ln, alpha=alpha, br=br),
        (res, g.reshape(1, d), b.reshape(1, d)), (row_spec, vec_spec, vec_spec),
        [jax.ShapeDtypeStruct((m, d), F32), jax.ShapeDtypeStruct((m, d), BF16)],
        [row_spec, row_spec], "proj_ln_ws")


def _conv_in_body(x_ref, wb_ref, wc_ref, wh_ref, cw_ref, prev_ref, v_ref, st_ref, conv_ref, *, seq, nseq):
    x = x_ref[...]
    b = _dot(x, wb_ref[...])
    u = _dot(x, wc_ref[...]) * _dot(x, wh_ref[...])
    w0 = cw_ref[0:1, :]
    w1 = cw_ref[1:2, :]
    w2 = cw_ref[2:3, :]
    for s in range(nseq):
        us = u[s * seq:(s + 1) * seq]
        tt = lax.broadcasted_iota(jnp.int32, us.shape, 0)
        p0 = prev_ref[s, 0:1, :]
        p1 = prev_ref[s, 1:2, :]
        u1 = jnp.where(tt >= 1, pltpu.roll(us, 1, axis=0), p1)
        u2 = jnp.where(tt >= 2, pltpu.roll(us, 2, axis=0), jnp.where(tt == 0, p0, p1))
        conv_ref[s * seq:(s + 1) * seq, :] = u2 * w0 + u1 * w1 + us * w2
        st_ref[s] = us[seq - SUBLANES:seq]
    v_ref[...] = (b * conv_ref[...]).astype(v_ref.dtype)


def _conv_in(xb, w_in, conv_w, prev, seq):
    m, d = xb.shape
    nb_seq = m // seq
    nseq = 1 if seq >= 512 else nb_seq
    bm = nseq * seq
    bn = _tile(d, 256, LANES)
    nb = d // bn
    x_mode = pl.Buffered(1) if bm * d * 2 > (8 << 20) else None
    return pl.pallas_call(
        functools.partial(_conv_in_body, seq=seq, nseq=nseq),
        grid=(m // bm, nb),
        in_specs=[
            pl.BlockSpec((bm, d), lambda i, j: (i, 0), pipeline_mode=x_mode),
            pl.BlockSpec((d, bn), lambda i, j: (0, j)),
            pl.BlockSpec((d, bn), lambda i, j: (0, j + nb)),
            pl.BlockSpec((d, bn), lambda i, j: (0, j + 2 * nb)),
            pl.BlockSpec((CONV_W, bn), lambda i, j: (0, j)),
            pl.BlockSpec((nseq, CONV_W - 1, bn), lambda i, j: (i, 0, j)),
        ],
        out_specs=[
            pl.BlockSpec((bm, bn), lambda i, j: (i, j)),
            pl.BlockSpec((nseq, SUBLANES, bn), lambda i, j: (i, 0, j)),
        ],
        out_shape=[
            jax.ShapeDtypeStruct((m, d), BF16),
            jax.ShapeDtypeStruct((nb_seq, SUBLANES, d), F32),
        ],
        scratch_shapes=[pltpu.VMEM((bm, bn), F32)],
        compiler_params=_params("parallel", "arbitrary"),
        name="conv_in",
    )(xb, w_in, w_in, w_in, conv_w, prev)


def _proj_ln_body(lhs_ref, w_ref, res_ref, g_ref, b_ref, o_ref, ob_ref, *, nk, alpha, bn, br):
    k = pl.program_id(1)
    bm, d = o_ref.shape

    @pl.when(k == 0)
    def _():
        o_ref[...] = jnp.zeros_like(o_ref)

    lhs = lhs_ref[...]
    for c in range(d // bn):
        o_ref[:, c * bn:(c + 1) * bn] += _dot(lhs, w_ref[:, c * bn:(c + 1) * bn])

    @pl.when(k == nk - 1)
    def _():
        def rows(r, carry):
            sl = pl.ds(pl.multiple_of(r * br, br), br)
            y = _layer_norm(alpha * res_ref[sl, :] + o_ref[sl, :], g_ref[...], b_ref[...])
            o_ref[sl, :] = y
            ob_ref[sl, :] = y.astype(ob_ref.dtype)
            return carry

        lax.fori_loop(0, bm // br, rows, 0)


def _proj_ln(lhs, w, res, g, b, alpha):
    m, kdim = lhs.shape
    d = w.shape[-1]
    bm = _tile(m, 512, SUBLANES * 2)
    bk = _tile(kdim, 1024, LANES)
    nk = kdim // bk
    bn = _tile(d, 512, LANES)
    br = _tile(bm, 32, SUBLANES * 2)
    return pl.pallas_call(
        functools.partial(_proj_ln_body, nk=nk, alpha=alpha, bn=bn, br=br),
        grid=(m // bm, nk),
        in_specs=[
            pl.BlockSpec((bm, bk), lambda i, k: (i, k)),
            pl.BlockSpec((bk, d), lambda i, k: (k, 0)),
            pl.BlockSpec((bm, d), lambda i, k: (i, 0), pipeline_mode=pl.Buffered(1)),
            pl.BlockSpec((1, d), lambda i, k: (0, 0)),
            pl.BlockSpec((1, d), lambda i, k: (0, 0)),
        ],
        out_specs=[
            pl.BlockSpec((bm, d), lambda i, k: (i, 0)),
            pl.BlockSpec((bm, d), lambda i, k: (i, 0)),
        ],
        out_shape=[jax.ShapeDtypeStruct((m, d), F32), jax.ShapeDtypeStruct((m, d), BF16)],
        compiler_params=_params("parallel", "arbitrary"),
        name="proj_ln",
    )(lhs, w, res, g.reshape(1, d), b.reshape(1, d))


def _router_body(x_ref, w_ref, b_ref, o_ref, *, n_experts):
    logits = jnp.dot(x_ref[...], w_ref[...], preferred_element_type=F32,
                     precision=lax.Precision.HIGHEST) + b_ref[...]
    lane = lax.broadcasted_iota(jnp.int32, logits.shape, 1)
    lg = jnp.where(lane < n_experts, logits, -jnp.inf)
    m1 = jnp.max(lg, axis=-1, keepdims=True)
    i1 = jnp.min(jnp.where(lg == m1, lane, LANES), axis=-1, keepdims=True)
    lg2 = jnp.where(lane == i1, -jnp.inf, lg)
    m2 = jnp.max(lg2, axis=-1, keepdims=True)
    i2 = jnp.min(jnp.where(lg2 == m2, lane, LANES), axis=-1, keepdims=True)
    e2 = jnp.exp(m2 - m1)
    w1 = 1.0 / (1.0 + e2)
    w2 = e2 / (1.0 + e2)
    o_ref[...] = jnp.where(lane == 0, w1, jnp.where(lane == 1, w2, jnp.where(
        lane == 2, i1.astype(F32), jnp.where(lane == 3, i2.astype(F32), 0.0))))


def _router(x, w_router, b_router):
    m, d = x.shape
    n_experts = w_router.shape[-1]
    w = jnp.pad(w_router, ((0, 0), (0, LANES - n_experts)))
    b = jnp.pad(b_router, (0, LANES - n_experts)).reshape(1, LANES)
    bm = _tile(m, 512, SUBLANES)
    return pl.pallas_call(
        functools.partial(_router_body, n_experts=n_experts),
        grid=(m // bm,),
        in_specs=[
            pl.BlockSpec((bm, d), lambda i: (i, 0)),
            pl.BlockSpec((d, LANES), lambda i: (0, 0)),
            pl.BlockSpec((1, LANES), lambda i: (0, 0)),
        ],
        out_specs=pl.BlockSpec((bm, LANES), lambda i: (i, 0)),
        out_shape=jax.ShapeDtypeStruct((m, LANES), F32),
        compiler_params=_params("parallel"),
        name="router",
    )(x, w, b)


MOE_ROW_TILE = 512


def _route_meta(eid, n_exp, bm):
    m = eid.shape[0]
    a = 2 * m
    n_tiles = -(-a // bm) + n_exp
    e_flat = eid.reshape(a)
    order = jnp.argsort(e_flat, stable=True).astype(jnp.int32)
    e_sorted = e_flat[order]
    counts = jnp.sum(e_flat[:, None] == jnp.arange(n_exp, dtype=jnp.int32)[None, :], axis=0).astype(jnp.int32)
    tiles_e = (counts + bm - 1) // bm
    tile_end = jnp.cumsum(tiles_e).astype(jnp.int32)
    tile_start = tile_end - tiles_e
    first = jnp.cumsum(counts).astype(jnp.int32) - counts
    pos_sorted = tile_start[e_sorted] * bm + (jnp.arange(a, dtype=jnp.int32) - first[e_sorted])
    row_token = jnp.zeros((n_tiles * bm,), jnp.int32).at[pos_sorted].set(order // 2)
    pos = jnp.zeros((a,), jnp.int32).at[order].set(pos_sorted)
    tile_expert = jnp.minimum(jnp.searchsorted(tile_end, jnp.arange(n_tiles, dtype=jnp.int32), side="right"),
                              n_exp - 1).astype(jnp.int32)
    return dict(row_token=row_token, pos=pos, tile_expert=tile_expert, n_used=tile_end[-1:], n_tiles=n_tiles)


def _row_copy(src_hbm, row, dst, dst_row, sem):
    return pltpu.make_async_copy(src_hbm.at[pl.ds(row, 1), :], dst.at[pl.ds(dst_row, 1), :], sem)


def _moe_gather_body(tok_ref, x_hbm, o_ref, buf, sem, *, bm):
    base = pl.program_id(0) * bm

    def issue(r, carry):
        _row_copy(x_hbm, tok_ref[base + r], buf, r, sem).start()
        return carry

    def drain(r, carry):
        _row_copy(x_hbm, 0, buf, r, sem).wait()
        return carry

    lax.fori_loop(0, bm, issue, 0, unroll=8)
    lax.fori_loop(0, bm, drain, 0, unroll=8)
    o_ref[...] = buf[...].astype(o_ref.dtype)


def _moe_gather(row_token, x_all):
    d = x_all.shape[1]
    rows = row_token.shape[0]
    bm = _tile(rows, 256, SUBLANES * 2)
    grid_spec = pltpu.PrefetchScalarGridSpec(
        num_scalar_prefetch=1,
        grid=(rows // bm,),
        in_specs=[pl.BlockSpec(memory_space=pl.ANY)],
        out_specs=pl.BlockSpec((bm, d), lambda i, tok: (i, 0)),
        scratch_shapes=[pltpu.VMEM((bm, d), F32), pltpu.SemaphoreType.DMA],
    )
    return pl.pallas_call(
        functools.partial(_moe_gather_body, bm=bm),
        grid_spec=grid_spec,
        out_shape=jax.ShapeDtypeStruct((rows, d), BF16),
        compiler_params=_params("arbitrary"),
        name="moe_gather",
    )(row_token, x_all)


def _moe_down(meta, h, w_down4, layer, bm):
    rows = h.shape[0]
    d = w_down4.shape[-1]
    bn = _tile(d, 1024, LANES)
    return _ws_matmul(
        h, w_down4, layer, (0,), bn, d // bn, d // bn, bm, _epi_copy, (), (),
        [jax.ShapeDtypeStruct((rows, d), F32)],
        [pl.BlockSpec((bm, bn), lambda j, i, te, nu: (i, j))], "moe_down", meta)[0]


def _moe_combine_body(pos_ref, y_hbm, w_ref, res_ref, g_ref, b_ref, o_ref, ob_ref, buf, sem, *, bm, br, alpha):
    base = pl.program_id(0) * bm

    def issue(r, carry):
        for k in range(2):
            _row_copy(y_hbm, pos_ref[2 * (base + r) + k], buf.at[k], r, sem).start()
        return carry

    def drain(r, carry):
        for k in range(2):
            _row_copy(y_hbm, 0, buf.at[k], r, sem).wait()
        return carry

    lax.fori_loop(0, bm, issue, 0, unroll=8)
    lax.fori_loop(0, bm, drain, 0, unroll=8)

    def rows(r, carry):
        sl = pl.ds(pl.multiple_of(r * br, br), br)
        f = w_ref[sl, 0:1] * buf[0, sl, :] + w_ref[sl, 1:2] * buf[1, sl, :]
        y = _layer_norm(alpha * res_ref[sl, :] + f, g_ref[...], b_ref[...])
        o_ref[sl, :] = y
        ob_ref[sl, :] = y.astype(ob_ref.dtype)
        return carry

    lax.fori_loop(0, bm // br, rows, 0)


def _moe_combine(pos, y, route, res, g, b, alpha):
    m, d = res.shape
    bm = _tile(m, 128, SUBLANES * 2)
    br = _tile(bm, 32, SUBLANES * 2)
    grid_spec = pltpu.PrefetchScalarGridSpec(
        num_scalar_prefetch=1,
        grid=(m // bm,),
        in_specs=[
            pl.BlockSpec(memory_space=pl.ANY),
            pl.BlockSpec((bm, LANES), lambda i, p: (i, 0)),
            pl.BlockSpec((bm, d), lambda i, p: (i, 0)),
            pl.BlockSpec((1, d), lambda i, p: (0, 0)),
            pl.BlockSpec((1, d), lambda i, p: (0, 0)),
        ],
        out_specs=[
            pl.BlockSpec((bm, d), lambda i, p: (i, 0)),
            pl.BlockSpec((bm, d), lambda i, p: (i, 0)),
        ],
        scratch_shapes=[pltpu.VMEM((2, bm, d), F32), pltpu.SemaphoreType.DMA],
    )
    return pl.pallas_call(
        functools.partial(_moe_combine_body, bm=bm, br=br, alpha=alpha),
        grid_spec=grid_spec,
        out_shape=[jax.ShapeDtypeStruct((m, d), F32), jax.ShapeDtypeStruct((m, d), BF16)],
        compiler_params=_params("arbitrary"),
        name="moe_combine",
    )(pos, y, route, res, g.reshape(1, d), b.reshape(1, d))


def _moe_layer(streams, layer, raw, w, ln_g, ln_b, alpha):
    n_exp = raw["moe_w_router"].shape[-1]
    routes = [_router(s["x"], raw["moe_w_router"][layer], raw["moe_b_router"][layer]) for s in streams]
    route_all = jnp.concatenate(routes, axis=0)
    x_all = jnp.concatenate([s["x"] for s in streams], axis=0)
    bm = MOE_ROW_TILE
    meta = _route_meta(route_all[:, 2:4].astype(jnp.int32), n_exp, bm)
    xs = _moe_gather(meta["row_token"], x_all)
    h = _swiglu_ws(xs, raw["moe_w_gu"], layer, bm, meta)
    y = _moe_down(meta, h, raw["moe_w_down"], layer, bm)
    off = 0
    for s, route in zip(streams, routes):
        pos = meta["pos"][2 * off:2 * (off + s["m"])]
        s["x"], s["xb"] = _moe_combine(pos, y, route, s["x"], ln_g, ln_b, alpha)
        off += s["m"]


def _ple_body(xb_ref, x_ref, p_ref, wp_ref, wg_ref, o_ref, ob_ref):
    gate = _sigmoid(_dot(xb_ref[...], wg_ref[...]))
    y = x_ref[...] + _dot(p_ref[...], wp_ref[...]) * gate
    o_ref[...] = y
    ob_ref[...] = y.astype(ob_ref.dtype)


def _ple(x, xb, pb, w_proj, w_gate):
    m, d = x.shape
    dp = pb.shape[-1]
    bm = _tile(m, 1024, SUBLANES * 2)
    bn = _tile(d, 512, LANES)
    return pl.pallas_call(
        _ple_body,
        grid=(m // bm, d // bn),
        in_specs=[
            pl.BlockSpec((bm, d), lambda i, j: (i, 0)),
            pl.BlockSpec((bm, bn), lambda i, j: (i, j)),
            pl.BlockSpec((bm, dp), lambda i, j: (i, 0)),
            pl.BlockSpec((dp, bn), lambda i, j: (0, j)),
            pl.BlockSpec((d, bn), lambda i, j: (0, j)),
        ],
        out_specs=[
            pl.BlockSpec((bm, bn), lambda i, j: (i, j)),
            pl.BlockSpec((bm, bn), lambda i, j: (i, j)),
        ],
        out_shape=[jax.ShapeDtypeStruct((m, d), F32), jax.ShapeDtypeStruct((m, d), BF16)],
        compiler_params=_params("parallel", "arbitrary"),
        name="ple",
    )(xb, x, pb, w_proj, w_gate)


def _heads_body(x_ref, w_ref, cos_ref, sin_ref, o_ref, or_ref, *, heads_per_blk, rope_blocks, two_out):
    acc = _dot(x_ref[...], w_ref[...])
    cos = cos_ref[...]
    sin = sin_ref[...]

    def roped():
        return [_rope_head(acc[:, h * LANES:(h + 1) * LANES], cos, sin) for h in range(heads_per_blk)]

    if two_out:
        o_ref[...] = acc.astype(o_ref.dtype)
        for h, y in enumerate(roped()):
            or_ref[:, h * LANES:(h + 1) * LANES] = y.astype(or_ref.dtype)
    else:
        j = pl.program_id(1)
        is_rope = functools.reduce(jnp.logical_or, [j == rb for rb in rope_blocks])

        @pl.when(is_rope)
        def _():
            for h, y in enumerate(roped()):
                o_ref[:, h * LANES:(h + 1) * LANES] = y
                or_ref[:, h * LANES:(h + 1) * LANES] = y.astype(or_ref.dtype)

        @pl.when(jnp.logical_not(is_rope))
        def _():
            o_ref[...] = acc
            or_ref[...] = acc.astype(or_ref.dtype)


def _heads_proj(xb, w, cos, sin, *, two_out, rope_blocks, out_dtypes):
    m, d = xb.shape
    n = w.shape[-1]
    bn = 4 * LANES
    bm = _tile(m, 1024, SUBLANES * 2)
    return pl.pallas_call(
        functools.partial(_heads_body, heads_per_blk=bn // LANES, rope_blocks=rope_blocks, two_out=two_out),
        grid=(m // bm, n // bn),
        in_specs=[
            pl.BlockSpec((bm, d), lambda i, j: (i, 0)),
            pl.BlockSpec((d, bn), lambda i, j: (0, j)),
            pl.BlockSpec((bm, LANES), lambda i, j: (i, 0)),
            pl.BlockSpec((bm, LANES), lambda i, j: (i, 0)),
        ],
        out_specs=[
            pl.BlockSpec((bm, bn), lambda i, j: (i, j)),
            pl.BlockSpec((bm, bn), lambda i, j: (i, j)),
        ],
        out_shape=[jax.ShapeDtypeStruct((m, n), out_dtypes[0]), jax.ShapeDtypeStruct((m, n), out_dtypes[1])],
        compiler_params=_params("parallel", "arbitrary"),
        name="heads_proj",
    )(xb, w, cos, sin)


def _gates_body(x_ref, w_ref, o_ref):
    o_ref[...] = _sigmoid(_dot(x_ref[...], w_ref[...]))


def _gates_proj(xb, w):
    m, d = xb.shape
    n = w.shape[-1]
    bm = _tile(m, 1024, SUBLANES * 2)
    return pl.pallas_call(
        _gates_body,
        grid=(m // bm,),
        in_specs=[pl.BlockSpec((bm, d), lambda i: (i, 0)), pl.BlockSpec((d, n), lambda i: (0, 0))],
        out_specs=pl.BlockSpec((bm, n), lambda i: (i, 0)),
        out_shape=jax.ShapeDtypeStruct((m, n), F32),
        compiler_params=_params("parallel"),
        name="gates_proj",
    )(xb, w)


def _gather_body(pt_ref, x_ref, oc_ref, os_ref):
    half = oc_ref.shape[-1]
    oc_ref[...] = x_ref[:, :half]
    os_ref[...] = x_ref[:, half:].astype(os_ref.dtype)


def _gather_pages(cache2, page_table):
    n_pool, page, width = cache2.shape
    nb, n_pages = page_table.shape
    half = width // 2
    grid_spec = pltpu.PrefetchScalarGridSpec(
        num_scalar_prefetch=1,
        grid=(nb, n_pages),
        in_specs=[pl.BlockSpec((None, page, width), lambda b, p, pt: (pt[b, p], 0, 0))],
        out_specs=[
            pl.BlockSpec((None, page, half), lambda b, p, pt: (b, p, 0)),
            pl.BlockSpec((None, page, half), lambda b, p, pt: (b, p, 0)),
        ],
    )
    return pl.pallas_call(
        _gather_body,
        grid_spec=grid_spec,
        out_shape=[
            jax.ShapeDtypeStruct((nb, n_pages * page, half), F32),
            jax.ShapeDtypeStruct((nb, n_pages * page, half), BF16),
        ],
        compiler_params=_params("parallel", "arbitrary"),
        name="gather_pages",
    )(page_table, cache2)


def _compress_body(x_ref, pe_ref, w_ref, o_ref, *, nc):
    half = nc // 2
    acc = jnp.zeros((nc, LANES), F32)
    for l in range(CMP_BLOCK):
        xe = x_ref[pl.ds(l, half, stride=2 * CMP_BLOCK), :]
        xo = x_ref[pl.ds(l + CMP_BLOCK, half, stride=2 * CMP_BLOCK), :]
        xl = jnp.concatenate([xe, xo], axis=0) + pe_ref[l:l + 1, :]
        acc = acc + _dot(xl.astype(BF16), w_ref[l])
    o_ref[...] = acc.astype(o_ref.dtype)


def _compress(x3, pe, w, n_kv, nc):
    nb = x3.shape[0]
    tc = nc * CMP_BLOCK
    return pl.pallas_call(
        functools.partial(_compress_body, nc=nc),
        grid=(nb, 2, n_kv),
        in_specs=[
            pl.BlockSpec((None, tc, LANES), lambda b, c, g: (b, 0, c * n_kv + g)),
            pl.BlockSpec((None, CMP_BLOCK, LANES), lambda b, c, g: (c, 0, 0)),
            pl.BlockSpec((None, CMP_BLOCK, LANES, LANES), lambda b, c, g: (c, 0, 0, 0)),
        ],
        out_specs=pl.BlockSpec((None, None, None, nc, LANES), lambda b, c, g: (c, b, g, 0, 0)),
        out_shape=jax.ShapeDtypeStruct((2, nb, n_kv, nc, LANES), BF16),
        compiler_params=_params("parallel", "parallel", "arbitrary"),
        name="compress",
    )(x3, pe, w)


def _cmp_block_of_lane(nc, shape):
    lane = lax.broadcasted_iota(jnp.int32, shape, len(shape) - 1)
    half = nc // 2
    return jnp.where(lane < half, 2 * lane, 2 * (lane - half) + 1)


def _cmp_branch(q, kc, vc, trow, nc, scale):
    s = _dot_nt(q, kc) * scale
    n = _cmp_block_of_lane(nc, s.shape)
    ok = ((n + 1) * CMP_BLOCK - 1) <= trow
    sm = jnp.where(ok, s, NEG)
    e = jnp.exp(sm - jnp.max(sm, axis=-1, keepdims=True))
    p = jnp.where(ok, e / jnp.sum(e, axis=-1, keepdims=True), 0.0)
    return p, _dot(p.astype(BF16), vc)


def _online_update(carry, s, ok, v):
    m, l, acc = carry
    sm = jnp.where(ok, s, NEG)
    m_new = jnp.maximum(m, jnp.max(sm, axis=-1, keepdims=True))
    a = jnp.exp(m - m_new)
    p = jnp.where(ok, jnp.exp(sm - m_new), 0.0)
    return m_new, a * l + jnp.sum(p, axis=-1, keepdims=True), a * acc + _dot(p.astype(BF16), v)


def _online_init(rows):
    return (jnp.full((rows, 1), NEG, F32), jnp.zeros((rows, 1), F32), jnp.zeros((rows, LANES), F32))


def _block_mask(sel_rows, k0, tk):
    nbk = sel_rows.shape[-1]
    blk = lax.broadcasted_iota(jnp.int32, (nbk, tk), 0)
    kpos = k0 + lax.broadcasted_iota(jnp.int32, (nbk, tk), 1)
    expand = (blk == kpos // SEL_BLOCK).astype(BF16)
    return _dot(sel_rows, expand)


def _attn_prompt_body(q_ref, qr_ref, gt_ref, kc_ref, vc_ref, ks_ref, vs_ref, kw_ref, vw_ref, o_ref,
                      *, n_rep, tq, seq, nc, nsb, scale):
    t0 = pl.program_id(2) * tq
    rows = n_rep * tq
    q = jnp.concatenate([q_ref[:, r * LANES:(r + 1) * LANES] for r in range(n_rep)], axis=0)
    qr = jnp.concatenate([qr_ref[:, r * LANES:(r + 1) * LANES] for r in range(n_rep)], axis=0)
    trow = t0 + lax.rem(lax.broadcasted_iota(jnp.int32, (rows, 1), 0), tq)

    p_cmp, o_cmp = _cmp_branch(q, kc_ref[...], vc_ref[...], trow, nc, scale)
    imp = p_cmp[0:tq]
    for r in range(1, n_rep):
        imp = imp + p_cmp[r * tq:(r + 1) * tq]
    half = nc // 2
    imp = imp[:, :half] + imp[:, half:nc]
    imp = jnp.concatenate([imp, jnp.zeros((tq, LANES - half), F32)], axis=1)
    imp_t = imp.T[:nsb]

    j = lax.broadcasted_iota(jnp.int32, (nsb, tq), 0)
    tcol = t0 + lax.broadcasted_iota(jnp.int32, (nsb, tq), 1)
    cur = tcol // SEL_BLOCK
    forced = (j == 0) | (j == cur) | (j == cur - 1)
    avail = j * SEL_BLOCK <= tcol
    work = jnp.where(avail, imp_t + FORCE * forced.astype(F32), -1.0)
    sel = jnp.zeros((nsb, tq), F32)
    for _ in range(min(N_SEL, nsb)):
        mx = jnp.max(work, axis=0, keepdims=True)
        idx = jnp.min(jnp.where(work == mx, j, nsb), axis=0, keepdims=True)
        pick = j == idx
        sel = jnp.where(pick, 1.0, sel)
        work = jnp.where(pick, -jnp.inf, work)
    sel = jnp.where(avail, sel, 0.0)
    sel = jnp.concatenate([sel, jnp.zeros((LANES - nsb, tq), F32)], axis=0).T
    sel_rows = jnp.concatenate([sel.astype(BF16)] * n_rep, axis=0)

    tk = min(512, seq)
    n_tiles = (t0 + tq + tk - 1) // tk

    def sel_step(kt, carry):
        k0 = pl.multiple_of(kt * tk, tk)
        s = _dot_nt(qr, ks_ref[pl.ds(k0, tk), :]) * scale
        kpos = k0 + lax.broadcasted_iota(jnp.int32, (rows, tk), 1)
        ok = (_block_mask(sel_rows, k0, tk) > 0.5) & (kpos <= trow)
        return _online_update(carry, s, ok, vs_ref[pl.ds(k0, tk), :])

    _, l_sel, acc_sel = lax.fori_loop(0, n_tiles, sel_step, _online_init(rows))
    o_sel = acc_sel / l_sel

    wl = min(WINDOW + tq, seq)
    w0 = pl.multiple_of(jnp.maximum(t0 + tq - wl, 0), tq)
    s = _dot_nt(qr, kw_ref[pl.ds(w0, wl), :]) * scale
    dist = trow - (w0 + lax.broadcasted_iota(jnp.int32, (rows, wl), 1))
    ok = (dist >= 0) & (dist < WINDOW)
    _, l_win, acc_win = _online_update(_online_init(rows), s, ok, vw_ref[pl.ds(w0, wl), :])
    o_win = acc_win / l_win

    for r in range(n_rep):
        sl = slice(r * tq, (r + 1) * tq)
        o = (gt_ref[:, r:r + 1] * o_cmp[sl]
             + gt_ref[:, n_rep + r:n_rep + r + 1] * o_sel[sl]
             + gt_ref[:, 2 * n_rep + r:2 * n_rep + r + 1] * o_win[sl])
        o_ref[:, r * LANES:(r + 1) * LANES] = o.astype(o_ref.dtype)


def _attn_prompt(q, qr, gates_g, cmp_kv, kvb, nb, seq, n_kv):
    m, d = q.shape
    n_rep = d // (n_kv * LANES)
    nc = seq // CMP_BLOCK
    nsb = -(-seq // SEL_BLOCK)
    tq = _tile(seq, 128, LANES)
    nq = seq // tq
    hw = n_rep * LANES
    kv_spec = lambda part: pl.BlockSpec((seq, LANES), lambda b, g, i: (b, part * n_kv + g))
    cmp_spec = lambda c: pl.BlockSpec((None, None, None, nc, LANES), lambda b, g, i: (c, b, g, 0, 0))
    return pl.pallas_call(
        functools.partial(_attn_prompt_body, n_rep=n_rep, tq=tq, seq=seq, nc=nc, nsb=nsb,
                          scale=float(LANES) ** -0.5),
        grid=(nb, n_kv, nq),
        in_specs=[
            pl.BlockSpec((tq, hw), lambda b, g, i: (b * nq + i, g)),
            pl.BlockSpec((tq, hw), lambda b, g, i: (b * nq + i, g)),
            pl.BlockSpec((None, tq, 3 * n_rep), lambda b, g, i: (g, b * nq + i, 0)),
            cmp_spec(0), cmp_spec(1),
            kv_spec(2), kv_spec(3), kv_spec(4), kv_spec(5),
        ],
        out_specs=pl.BlockSpec((tq, hw), lambda b, g, i: (b * nq + i, g)),
        out_shape=jax.ShapeDtypeStruct((m, d), BF16),
        compiler_params=_params("parallel", "parallel", "arbitrary"),
        name="attn_prompt",
    )(q, qr, gates_g, cmp_kv, cmp_kv, kvb, kvb, kvb, kvb)


def _pad_rows(a, rows):
    return jnp.concatenate([a, jnp.zeros((rows - a.shape[0], a.shape[1]), a.dtype)], axis=0)


def _attn_sample_body(q_ref, qr_ref, gt_ref, kc_ref, vc_ref, ksp_ref, vsp_ref, ksn_ref, vsn_ref,
                      kwp_ref, vwp_ref, kwn_ref, vwn_ref, o_ref,
                      *, n_rep, tn, past, nc, nsb, nbk, scale):
    rows = n_rep * tn
    q = jnp.concatenate([q_ref[:, r * LANES:(r + 1) * LANES] for r in range(n_rep)], axis=0).astype(BF16)
    qr = jnp.concatenate([qr_ref[:, r * LANES:(r + 1) * LANES] for r in range(n_rep)], axis=0).astype(BF16)
    trow = past + lax.rem(lax.broadcasted_iota(jnp.int32, (rows, 1), 0), tn)

    p_cmp, o_cmp = _cmp_branch(q, kc_ref[...], vc_ref[...], trow, nc, scale)
    imp = p_cmp[0:tn]
    for r in range(1, n_rep):
        imp = imp + p_cmp[r * tn:(r + 1) * tn]
    half = nc // 2
    imp = imp[:, :half] + imp[:, half:nc]
    imp = jnp.concatenate([imp, jnp.zeros((tn, nbk - half), F32)], axis=1)

    j = lax.broadcasted_iota(jnp.int32, (tn, nbk), 1)
    tcol = past + lax.broadcasted_iota(jnp.int32, (tn, nbk), 0)
    cur = tcol // SEL_BLOCK
    forced = (j == 0) | (j == cur) | (j == cur - 1)
    avail = (j * SEL_BLOCK <= tcol) & (j < nsb)
    work = jnp.where(avail, imp + FORCE * forced.astype(F32), -1.0)
    work = jnp.where(j < nsb, work, -jnp.inf)
    sel = jnp.zeros((tn, nbk), F32)
    for _ in range(min(N_SEL, nsb)):
        mx = jnp.max(work, axis=1, keepdims=True)
        idx = jnp.min(jnp.where(work == mx, j, nbk), axis=1, keepdims=True)
        pick = j == idx
        sel = jnp.where(pick, 1.0, sel)
        work = jnp.where(pick, -jnp.inf, work)
    sel = jnp.where(avail, sel, 0.0)
    sel_rows = jnp.concatenate([sel.astype(BF16)] * n_rep, axis=0)

    tk = min(1024, past)

    def sel_step(kt, carry):
        k0 = pl.multiple_of(kt * tk, tk)
        s = _dot_nt(qr, ksp_ref[pl.ds(k0, tk), :]) * scale
        kpos = k0 + lax.broadcasted_iota(jnp.int32, (rows, tk), 1)
        ok = (_block_mask(sel_rows, k0, tk) > 0.5) & (kpos <= trow)
        return _online_update(carry, s, ok, vsp_ref[pl.ds(k0, tk), :])

    carry = lax.fori_loop(0, past // tk, sel_step, _online_init(rows))
    kn = _pad_rows(ksn_ref[...], LANES).astype(BF16)
    vn = _pad_rows(vsn_ref[...], LANES).astype(BF16)
    kpos = past + lax.broadcasted_iota(jnp.int32, (rows, LANES), 1)
    ok = (_block_mask(sel_rows, past, LANES) > 0.5) & (kpos <= trow)
    _, l_sel, acc_sel = _online_update(carry, _dot_nt(qr, kn) * scale, ok, vn)
    o_sel = acc_sel / l_sel

    wb = kwp_ref.shape[0]
    dist = trow - (past - wb + lax.broadcasted_iota(jnp.int32, (rows, wb), 1))
    ok = (dist >= 0) & (dist < WINDOW)
    carry = _online_update(_online_init(rows), _dot_nt(qr, kwp_ref[...].astype(BF16)) * scale, ok,
                           vwp_ref[...].astype(BF16))
    kn = _pad_rows(kwn_ref[...], LANES).astype(BF16)
    vn = _pad_rows(vwn_ref[...], LANES).astype(BF16)
    dist = trow - kpos
    ok = (dist >= 0) & (dist < WINDOW)
    _, l_win, acc_win = _online_update(carry, _dot_nt(qr, kn) * scale, ok, vn)
    o_win = acc_win / l_win

    for r in range(n_rep):
        sl = slice(r * tn, (r + 1) * tn)
        o = (gt_ref[:, r:r + 1] * o_cmp[sl]
             + gt_ref[:, n_rep + r:n_rep + r + 1] * o_sel[sl]
             + gt_ref[:, 2 * n_rep + r:2 * n_rep + r + 1] * o_win[sl])
        o_ref[:, r * LANES:(r + 1) * LANES] = o


def _attn_sample(q, qr, gates_g, cmp_kv, sel_past, kv_new, win_past, nb, tn, n_kv):
    m, d = q.shape
    n_rep = d // (n_kv * LANES)
    past = sel_past.shape[1]
    nc = cmp_kv.shape[3]
    nsb = -(-(past + tn) // SEL_BLOCK)
    nbk = -(-nsb // LANES) * LANES
    wb = win_past.shape[1]
    hw = n_rep * LANES
    new_spec = lambda part: pl.BlockSpec((tn, LANES), lambda b, g: (b, part * n_kv + g))
    cmp_spec = lambda c: pl.BlockSpec((None, None, None, nc, LANES), lambda b, g: (c, b, g, 0, 0))
    return pl.pallas_call(
        functools.partial(_attn_sample_body, n_rep=n_rep, tn=tn, past=past, nc=nc, nsb=nsb, nbk=nbk,
                          scale=float(LANES) ** -0.5),
        grid=(nb, n_kv),
        in_specs=[
            pl.BlockSpec((tn, hw), lambda b, g: (b, g)),
            pl.BlockSpec((tn, hw), lambda b, g: (b, g)),
            pl.BlockSpec((None, tn, 3 * n_rep), lambda b, g: (g, b, 0)),
            cmp_spec(0), cmp_spec(1),
            pl.BlockSpec((None, past, LANES), lambda b, g: (b, 0, g)),
            pl.BlockSpec((None, past, LANES), lambda b, g: (b, 0, n_kv + g)),
            new_spec(2), new_spec(3),
            pl.BlockSpec((None, wb, LANES), lambda b, g: (b, 0, g)),
            pl.BlockSpec((None, wb, LANES), lambda b, g: (b, 0, n_kv + g)),
            new_spec(4), new_spec(5),
        ],
        out_specs=pl.BlockSpec((tn, hw), lambda b, g: (b, g)),
        out_shape=jax.ShapeDtypeStruct((m, d), F32),
        compiler_params=_params("parallel", "arbitrary"),
        name="attn_sample",
    )(q, qr, gates_g, cmp_kv, cmp_kv, sel_past, sel_past, kv_new, kv_new, win_past, win_past, kv_new, kv_new)


def _rope_tables(pos):
    half = LANES // 2
    inv = ROPE_THETA ** (-jnp.arange(half, dtype=F32) / half)
    ang = pos.astype(F32)[:, None] * inv[None, :]
    cos, sin = jnp.cos(ang), jnp.sin(ang)
    return jnp.concatenate([cos, cos], axis=-1), jnp.concatenate([-sin, sin], axis=-1)


def _prep_weights(a_w_in, w_kv, cmp_pe, cmp_w, b_w_qg, ffn_w_down, ple_w_proj, ple_w_gate, n_kv):
    d = a_w_in.shape[1]
    n_heads = d // LANES
    n_rep = n_heads // n_kv
    f = ffn_w_down.shape[1]
    fp = -(-f // 1024) * 1024
    w = {}
    w["a_w_in"] = a_w_in.astype(BF16)
    w["w_kv"] = w_kv.astype(BF16)
    w["cmp_pe"] = jnp.transpose(cmp_pe, (1, 0, 2))
    w["cmp_w"] = jnp.transpose(cmp_w, (1, 0, 2, 3)).astype(BF16)
    w["w_q"] = b_w_qg[:, :, :d].astype(BF16)
    wg = b_w_qg[:, :, d:].reshape(-1, d, n_kv, n_rep, 3)
    wg = jnp.transpose(wg, (0, 1, 2, 4, 3)).reshape(-1, d, 3 * n_heads)
    w["w_gates"] = jnp.pad(wg, ((0, 0), (0, 0), (0, LANES - 3 * n_heads))).astype(BF16)
    w["ffn_w_down"] = jnp.pad(ffn_w_down, ((0, 0), (0, fp - f), (0, 0))).astype(BF16)
    w["ffn_f"] = fp
    w["ple_w_proj"] = ple_w_proj.astype(BF16)
    w["ple_w_gate"] = ple_w_gate.astype(BF16)
    return w


def _new_stream(x3, p4, conv_prev, past_ctx):
    nb, seq, d = x3.shape
    m = nb * seq
    past = 0 if past_ctx is None else past_ctx["past"]
    cos1, sin1 = _rope_tables(past + jnp.arange(seq, dtype=jnp.int32))
    x = x3.reshape(m, d)
    return dict(nb=nb, seq=seq, m=m, x=x, xb=x.astype(BF16), pb=p4.reshape(p4.shape[0], m, -1).astype(BF16),
                conv_prev=conv_prev, past_ctx=past_ctx, cos=jnp.tile(cos1, (nb, 1)), sin=jnp.tile(sin1, (nb, 1)),
                conv_states=[], ctx=None)


def _conv_mixer(s, i, w, raw, ln_g, ln_b, alpha):
    d = s["x"].shape[1]
    prev = jnp.zeros((s["nb"], CONV_W - 1, d), F32) if s["conv_prev"] is None else s["conv_prev"][i]
    v, st = _conv_in(s["xb"], w["a_w_in"][i], raw["a_conv_w"][i], prev, s["seq"])
    s["conv_states"].append(st[:, SUBLANES - (CONV_W - 1):])
    s["x"], s["xb"] = _proj_ln_ws(v, raw["a_w_out"][:, None], i, s["x"], ln_g, ln_b, alpha)


def _nsa_mixer(s, li, w, raw, ln_g, ln_b, alpha, n_kv):
    m, d = s["x"].shape
    n_heads = d // LANES
    n_rep = n_heads // n_kv
    prompt = s["past_ctx"] is None
    qdt = BF16 if prompt else F32
    q, qr = _heads_proj(s["xb"], w["w_q"][li], s["cos"], s["sin"], two_out=True, rope_blocks=(),
                        out_dtypes=(qdt, qdt))
    gates = _gates_proj(s["xb"], w["w_gates"][li])[:, :3 * n_heads]
    gates_g = jnp.transpose(gates.reshape(m, n_kv, 3 * n_rep), (1, 0, 2))
    ctx = s["ctx"]
    if prompt:
        o = _attn_prompt(q, qr, gates_g, ctx["cmp_kv"], ctx["kvb"], s["nb"], s["seq"], n_kv)
    else:
        o = _attn_sample(q, qr, gates_g, ctx["cmp_kv"], s["past_ctx"]["sel_past"], ctx["kv_all"],
                         s["past_ctx"]["win_past"], s["nb"], s["seq"], n_kv).astype(BF16)
    s["x"], s["xb"] = _proj_ln_ws(o, raw["b_w_o"][:, None], li, s["x"], ln_g, ln_b, alpha)


def _shared_context(s, w, n_kv):
    kv_all, kvb = _heads_proj(s["xb"], w["w_kv"], s["cos"], s["sin"], two_out=False, rope_blocks=(2, 4),
                              out_dtypes=(F32, BF16))
    if s["past_ctx"] is None:
        cmp_kv = _compress(kv_all.reshape(s["nb"], s["seq"], -1), w["cmp_pe"], w["cmp_w"], n_kv,
                           s["seq"] // CMP_BLOCK)
    else:
        past = s["past_ctx"]["past"]
        assert past % CMP_BLOCK == 0 and s["seq"] < CMP_BLOCK
        cmp_kv = _compress(s["past_ctx"]["cmp_past"], w["cmp_pe"], w["cmp_w"], n_kv, past // CMP_BLOCK)
    s["ctx"] = dict(kv_all=kv_all, kvb=kvb, cmp_kv=cmp_kv)


def _trunk(streams, w, raw, n_kv):
    depth = raw["ln_g"].shape[0]
    n_a = depth // 2
    alpha = float((2 * depth) ** 0.25)
    for i in range(depth):
        ln_g, ln_b = raw["ln_g"][i], raw["ln_b"][i]
        for s in streams:
            if i < n_a:
                _conv_mixer(s, i, w, raw, ln_g[0], ln_b[0], alpha)
            else:
                _nsa_mixer(s, i - n_a, w, raw, ln_g[0], ln_b[0], alpha, n_kv)
        if i % 2 == 0:
            for s in streams:
                h = _swiglu_ws(s["xb"], raw["ffn_w_gu"][:, None], i // 2, _tile(s["m"], 1024, SUBLANES * 2),
                               f_pad=w["ffn_f"])
                s["x"], s["xb"] = _proj_ln(h, w["ffn_w_down"][i // 2], s["x"], ln_g[1], ln_b[1], alpha)
        else:
            _moe_layer(streams, i // 2, raw, w, ln_g[1], ln_b[1], alpha)
        for s in streams:
            s["x"], s["xb"] = _ple(s["x"], s["xb"], s["pb"][i], w["ple_w_proj"][i], w["ple_w_gate"][i])
            if i == n_a - 1:
                _shared_context(s, w, n_kv)
    outs = []
    for s in streams:
        nb, seq, hd_blk = s["nb"], s["seq"], n_kv * LANES
        kv_all = s["ctx"]["kv_all"]
        rows = kv_all[:, :4 * hd_blk].reshape(nb, seq, 4, n_kv, LANES)
        win = kv_all[:, 4 * hd_blk:].reshape(nb, seq, 2, n_kv, LANES)
        outs.append((s["x"].reshape(nb, seq, -1), rows, win, jnp.stack(s["conv_states"])))
    return outs


def kernel(x_prompt, x_sample, cache_kv, cache_win, state_conv, page_table, p_prompt, p_sample,
           a_w_in, a_conv_w, a_w_out, w_kv, cmp_pe, cmp_w, b_w_qg, b_w_o, ffn_w_gu, ffn_w_down,
           moe_w_router, moe_b_router, moe_w_gu, moe_w_down, ple_w_proj, ple_w_gate, ln_g, ln_b):
    n_kv = cache_kv.shape[3]
    raw = dict(a_conv_w=a_conv_w, a_w_out=a_w_out, b_w_o=b_w_o, ffn_w_gu=ffn_w_gu, moe_w_router=moe_w_router,
               moe_b_router=moe_b_router, moe_w_gu=moe_w_gu, moe_w_down=moe_w_down, ln_g=ln_g, ln_b=ln_b)
    w = _prep_weights(a_w_in, w_kv, cmp_pe, cmp_w, b_w_qg, ffn_w_down, ple_w_proj, ple_w_gate, n_kv)

    n_pool, page = cache_kv.shape[:2]
    cmp_past, sel_past = _gather_pages(cache_kv.reshape(n_pool, page, -1), page_table)
    nb_s, wb = cache_win.shape[:2]
    past_ctx = dict(past=page_table.shape[1] * page, cmp_past=cmp_past, sel_past=sel_past,
                    win_past=cache_win.reshape(nb_s, wb, -1))

    streams = [_new_stream(x_prompt, p_prompt, None, None),
               _new_stream(x_sample, p_sample, state_conv, past_ctx)]
    (y_p, rows_p, win_p, conv_p), (y_s, rows_s, win_s, conv_s) = _trunk(streams, w, raw, n_kv)
    win_state_p = win_p[:, -min(WINDOW, x_prompt.shape[1]):]
    win_state_s = jnp.concatenate([cache_win.astype(win_s.dtype), win_s], axis=1)[:, -wb:]

    return (y_p, y_s, rows_p, rows_s, win_state_p, win_state_s, conv_p, conv_s)
```

```python
import functools

import jax
import jax.numpy as jnp
from jax import lax
from jax.experimental import pallas as pl
from jax.experimental.pallas import tpu as pltpu

F32 = jnp.float32
BF16 = jnp.bfloat16

CONV_W = 3
CMP_BLOCK = 32
SEL_BLOCK = 64
N_SEL = 16
WINDOW = 512
ROPE_THETA = 10000.0
LN_EPS = 1e-5
NEG = -1e30
FORCE = 1e4

LANES = 128
SUBLANES = 8
VMEM_LIMIT_BYTES = 56 * 1024 * 1024


def _params(*sem):
    return pltpu.CompilerParams(dimension_semantics=sem, vmem_limit_bytes=VMEM_LIMIT_BYTES)


def _tile(n, pref, align):
    if n <= pref:
        return n
    t = (pref // align) * align
    while t >= align:
        if n % t == 0:
            return t
        t -= align
    raise ValueError(f"no tile for {n} (pref {pref}, align {align})")


def _dot(a, b):
    return jnp.dot(a, b, preferred_element_type=F32)


def _dot_nt(a, b):
    return lax.dot_general(a, b, (((1,), (1,)), ((), ())), preferred_element_type=F32)


def _sigmoid(x):
    return 1.0 / (1.0 + jnp.exp(-x))


def _rope_head(a, cos, sin_signed):
    return a * cos + pltpu.roll(a, a.shape[-1] // 2, axis=1) * sin_signed


def _layer_norm(z, g, b):
    mu = jnp.mean(z, axis=-1, keepdims=True)
    zc = z - mu
    var = jnp.mean(zc * zc, axis=-1, keepdims=True)
    return zc * lax.rsqrt(var + LN_EPS) * g + b


WS_STAGE_BYTES = 4 * 1024 * 1024


def _ws_body(te_ref, nu_ref, x_ref, w_hbm, *rest, layer, seg_offsets, bn, ck, nj_valid, epilogue, n_extra, n_out):
    extra = rest[:n_extra]
    outs = rest[n_extra:n_extra + n_out]
    cache, stage, acc_ref, sem = rest[n_extra + n_out:]
    j = pl.program_id(0)
    i = pl.program_id(1)
    e = te_ref[i]
    nseg = len(seg_offsets)
    n_chunks = x_ref.shape[1] // ck
    col_ok = j < nj_valid
    valid = jnp.logical_and(i < nu_ref[0], col_ok)
    changed = jnp.logical_and(jnp.logical_or(i == 0, e != te_ref[jnp.maximum(i - 1, 0)]), col_ok)

    def copies(c, slot):
        return [pltpu.make_async_copy(
            w_hbm.at[layer, e, pl.ds(c * ck, ck), pl.ds(pl.multiple_of(off + j * bn, LANES), bn)],
            stage.at[slot, s], sem.at[slot]) for s, off in enumerate(seg_offsets)]

    @pl.when(changed)
    def _():
        for cp in copies(0, 0):
            cp.start()
        for c in range(n_chunks):
            slot = c % 2
            if c + 1 < n_chunks:
                for cp in copies(c + 1, 1 - slot):
                    cp.start()
            for cp in copies(c, slot):
                cp.wait()
            for s in range(nseg):
                cache[c * ck:(c + 1) * ck, s * bn:(s + 1) * bn] = stage[slot, s].astype(BF16)
            part = _dot(x_ref[:, c * ck:(c + 1) * ck], cache[c * ck:(c + 1) * ck, :])
            if c == 0:
                acc_ref[...] = part
            else:
                acc_ref[...] += part

    @pl.when(jnp.logical_and(valid, jnp.logical_not(changed)))
    def _():
        acc_ref[...] = _dot(x_ref[...], cache[...])

    @pl.when(valid)
    def _():
        epilogue(acc_ref, extra, outs)

    @pl.when(jnp.logical_not(valid))
    def _():
        for o in outs:
            o[...] = jnp.zeros_like(o)


def _ws_matmul(x, w4, layer, seg_offsets, bn, nj, nj_valid, bm, epilogue, extra, extra_specs, out_shape, out_specs,
               name, meta=None):
    rows, k = x.shape
    n_tiles = rows // bm
    nseg = len(seg_offsets)
    ck = _tile(k, max(LANES, WS_STAGE_BYTES // (2 * nseg * bn * 4)), LANES)
    if meta is None:
        te = jnp.zeros((n_tiles,), jnp.int32)
        nu = jnp.full((1,), n_tiles, jnp.int32)
    else:
        te, nu = meta["tile_expert"], meta["n_used"]
    grid_spec = pltpu.PrefetchScalarGridSpec(
        num_scalar_prefetch=2,
        grid=(nj, n_tiles),
        in_specs=[pl.BlockSpec((bm, k), lambda j, i, te, nu: (i, 0)), pl.BlockSpec(memory_space=pl.ANY)]
        + list(extra_specs),
        out_specs=out_specs,
        scratch_shapes=[pltpu.VMEM((k, nseg * bn), BF16), pltpu.VMEM((2, nseg, ck, bn), F32),
                        pltpu.VMEM((bm, nseg * bn), F32), pltpu.SemaphoreType.DMA((2,))],
    )
    return pl.pallas_call(
        functools.partial(_ws_body, layer=layer, seg_offsets=tuple(seg_offsets), bn=bn, ck=ck, nj_valid=nj_valid,
                          epilogue=epilogue, n_extra=len(extra), n_out=len(out_specs)),
        grid_spec=grid_spec,
        out_shape=out_shape,
        compiler_params=_params("arbitrary", "arbitrary"),
        name=name,
    )(te, nu, x, w4, *extra)


def _epi_swiglu(acc_ref, extra, outs):
    bn = outs[0].shape[-1]
    g = acc_ref[:, :bn]
    outs[0][...] = (g * _sigmoid(g) * acc_ref[:, bn:]).astype(outs[0].dtype)


def _epi_slabs(acc_ref, extra, outs):
    for s in range(outs[0].shape[1]):
        outs[0][:, s, :] = acc_ref[:, s * LANES:(s + 1) * LANES]


def _epi_ln(acc_ref, extra, outs, *, alpha, br):
    res_ref, g_ref, b_ref = extra
    o_ref, ob_ref = outs

    def rows(r, carry):
        sl = pl.ds(pl.multiple_of(r * br, br), br)
        y = _layer_norm(alpha * res_ref[sl, :] + acc_ref[sl, :], g_ref[...], b_ref[...])
        o_ref[sl, :] = y
        ob_ref[sl, :] = y.astype(ob_ref.dtype)
        return carry

    lax.fori_loop(0, o_ref.shape[0] // br, rows, 0)


def _swiglu_ws(x, w_gu4, layer, bm, meta=None, f_pad=None):
    rows = x.shape[0]
    f = w_gu4.shape[-1] // 2
    f_out = f if f_pad is None else f_pad
    bn = _tile(f, 1024, LANES) if f % 1024 == 0 else _tile(f, 256, LANES)
    return _ws_matmul(
        x, w_gu4, layer, (0, f), bn, f_out // bn, f // bn, bm, _epi_swiglu, (), (),
        [jax.ShapeDtypeStruct((rows, f_out), BF16)],
        [pl.BlockSpec((bm, bn), lambda j, i, te, nu: (i, j))], "swiglu_ws", meta)[0]


def _proj_ln_ws(lhs, w4, layer, res, g, b, alpha):
    m, d = res.shape
    bm = _tile(m, 128, SUBLANES * 2)
    br = _tile(bm, 32, SUBLANES * 2)
    row_spec = pl.BlockSpec((bm, d), lambda j, i, te, nu: (i, 0))
    vec_spec = pl.BlockSpec((1, d), lambda j, i, te, nu: (0, 0))
    return _ws_matmul(
        lhs, w4, layer, (0,), d, 1, 1, bm, functools.partial(_epi_ln, alpha=alpha, br=br),
        (res, g.reshape(1, d), b.reshape(1, d)), (row_spec, vec_spec, vec_spec),
        [jax.ShapeDtypeStruct((m, d), F32), jax.ShapeDtypeStruct((m, d), BF16)],
        [row_spec, row_spec], "proj_ln_ws")


def _conv_in_body(x_ref, wb_ref, wc_ref, wh_ref, cw_ref, prev_ref, v_ref, st_ref, conv_ref, *, seq, nseq):
    x = x_ref[...]
    b = _dot(x, wb_ref[...])
    u = _dot(x, wc_ref[...]) * _dot(x, wh_ref[...])
    w0 = cw_ref[0:1, :]
    w1 = cw_ref[1:2, :]
    w2 = cw_ref[2:3, :]
    for s in range(nseq):
        us = u[s * seq:(s + 1) * seq]
        tt = lax.broadcasted_iota(jnp.int32, us.shape, 0)
        p0 = prev_ref[s, 0:1, :]
        p1 = prev_ref[s, 1:2, :]
        u1 = jnp.where(tt >= 1, pltpu.roll(us, 1, axis=0), p1)
        u2 = jnp.where(tt >= 2, pltpu.roll(us, 2, axis=0), jnp.where(tt == 0, p0, p1))
        conv_ref[s * seq:(s + 1) * seq, :] = u2 * w0 + u1 * w1 + us * w2
        st_ref[s] = us[seq - SUBLANES:seq]
    v_ref[...] = (b * conv_ref[...]).astype(v_ref.dtype)


def _conv_in(xb, w_in, conv_w, prev, seq):
    m, d = xb.shape
    nb_seq = m // seq
    nseq = 1 if seq >= 512 else nb_seq
    bm = nseq * seq
    bn = _tile(d, 256, LANES)
    nb = d // bn
    x_mode = pl.Buffered(1) if bm * d * 2 > (8 << 20) else None
    return pl.pallas_call(
        functools.partial(_conv_in_body, seq=seq, nseq=nseq),
        grid=(m // bm, nb),
        in_specs=[
            pl.BlockSpec((bm, d), lambda i, j: (i, 0), pipeline_mode=x_mode),
            pl.BlockSpec((d, bn), lambda i, j: (0, j)),
            pl.BlockSpec((d, bn), lambda i, j: (0, j + nb)),
            pl.BlockSpec((d, bn), lambda i, j: (0, j + 2 * nb)),
            pl.BlockSpec((CONV_W, bn), lambda i, j: (0, j)),
            pl.BlockSpec((nseq, CONV_W - 1, bn), lambda i, j: (i, 0, j)),
        ],
        out_specs=[
            pl.BlockSpec((bm, bn), lambda i, j: (i, j)),
            pl.BlockSpec((nseq, SUBLANES, bn), lambda i, j: (i, 0, j)),
        ],
        out_shape=[
            jax.ShapeDtypeStruct((m, d), BF16),
            jax.ShapeDtypeStruct((nb_seq, SUBLANES, d), F32),
        ],
        scratch_shapes=[pltpu.VMEM((bm, bn), F32)],
        compiler_params=_params("parallel", "arbitrary"),
        name="conv_in",
    )(xb, w_in, w_in, w_in, conv_w, prev)


def _proj_ln_body(lhs_ref, w_ref, res_ref, g_ref, b_ref, o_ref, ob_ref, *, nk, alpha, bn, br):
    k = pl.program_id(1)
    bm, d = o_ref.shape

    @pl.when(k == 0)
    def _():
        o_ref[...] = jnp.zeros_like(o_ref)

    lhs = lhs_ref[...]
    for c in range(d // bn):
        o_ref[:, c * bn:(c + 1) * bn] += _dot(lhs, w_ref[:, c * bn:(c + 1) * bn])

    @pl.when(k == nk - 1)
    def _():
        def rows(r, carry):
            sl = pl.ds(pl.multiple_of(r * br, br), br)
            y = _layer_norm(alpha * res_ref[sl, :] + o_ref[sl, :], g_ref[...], b_ref[...])
            o_ref[sl, :] = y
            ob_ref[sl, :] = y.astype(ob_ref.dtype)
            return carry

        lax.fori_loop(0, bm // br, rows, 0)


def _proj_ln(lhs, w, res, g, b, alpha):
    m, kdim = lhs.shape
    d = w.shape[-1]
    bm = _tile(m, 512, SUBLANES * 2)
    bk = _tile(kdim, 1024, LANES)
    nk = kdim // bk
    bn = _tile(d, 512, LANES)
    br = _tile(bm, 32, SUBLANES * 2)
    return pl.pallas_call(
        functools.partial(_proj_ln_body, nk=nk, alpha=alpha, bn=bn, br=br),
        grid=(m // bm, nk),
        in_specs=[
            pl.BlockSpec((bm, bk), lambda i, k: (i, k)),
            pl.BlockSpec((bk, d), lambda i, k: (k, 0)),
            pl.BlockSpec((bm, d), lambda i, k: (i, 0), pipeline_mode=pl.Buffered(1)),
            pl.BlockSpec((1, d), lambda i, k: (0, 0)),
            pl.BlockSpec((1, d), lambda i, k: (0, 0)),
        ],
        out_specs=[
            pl.BlockSpec((bm, d), lambda i, k: (i, 0)),
            pl.BlockSpec((bm, d), lambda i, k: (i, 0)),
        ],
        out_shape=[jax.ShapeDtypeStruct((m, d), F32), jax.ShapeDtypeStruct((m, d), BF16)],
        compiler_params=_params("parallel", "arbitrary"),
        name="proj_ln",
    )(lhs, w, res, g.reshape(1, d), b.reshape(1, d))


def _router_body(x_ref, w_ref, b_ref, o_ref, *, n_experts):
    logits = jnp.dot(x_ref[...], w_ref[...], preferred_element_type=F32,
                     precision=lax.Precision.HIGHEST) + b_ref[...]
    lane = lax.broadcasted_iota(jnp.int32, logits.shape, 1)
    lg = jnp.where(lane < n_experts, logits, -jnp.inf)
    m1 = jnp.max(lg, axis=-1, keepdims=True)
    i1 = jnp.min(jnp.where(lg == m1, lane, LANES), axis=-1, keepdims=True)
    lg2 = jnp.where(lane == i1, -jnp.inf, lg)
    m2 = jnp.max(lg2, axis=-1, keepdims=True)
    i2 = jnp.min(jnp.where(lg2 == m2, lane, LANES), axis=-1, keepdims=True)
    e2 = jnp.exp(m2 - m1)
    w1 = 1.0 / (1.0 + e2)
    w2 = e2 / (1.0 + e2)
    o_ref[...] = jnp.where(lane == 0, w1, jnp.where(lane == 1, w2, jnp.where(
        lane == 2, i1.astype(F32), jnp.where(lane == 3, i2.astype(F32), 0.0))))


def _router(x, w_router, b_router):
    m, d = x.shape
    n_experts = w_router.shape[-1]
    w = jnp.pad(w_router, ((0, 0), (0, LANES - n_experts)))
    b = jnp.pad(b_router, (0, LANES - n_experts)).reshape(1, LANES)
    bm = _tile(m, 512, SUBLANES)
    return pl.pallas_call(
        functools.partial(_router_body, n_experts=n_experts),
        grid=(m // bm,),
        in_specs=[
            pl.BlockSpec((bm, d), lambda i: (i, 0)),
            pl.BlockSpec((d, LANES), lambda i: (0, 0)),
            pl.BlockSpec((1, LANES), lambda i: (0, 0)),
        ],
        out_specs=pl.BlockSpec((bm, LANES), lambda i: (i, 0)),
        out_shape=jax.ShapeDtypeStruct((m, LANES), F32),
        compiler_params=_params("parallel"),
        name="router",
    )(x, w, b)


MOE_ROW_TILE = 512


def _route_meta(eid, n_exp, bm):
    m = eid.shape[0]
    a = 2 * m
    n_tiles = -(-a // bm) + n_exp
    e_flat = eid.reshape(a)
    order = jnp.argsort(e_flat, stable=True).astype(jnp.int32)
    e_sorted = e_flat[order]
    counts = jnp.sum(e_flat[:, None] == jnp.arange(n_exp, dtype=jnp.int32)[None, :], axis=0).astype(jnp.int32)
    tiles_e = (counts + bm - 1) // bm
    tile_end = jnp.cumsum(tiles_e).astype(jnp.int32)
    tile_start = tile_end - tiles_e
    first = jnp.cumsum(counts).astype(jnp.int32) - counts
    pos_sorted = tile_start[e_sorted] * bm + (jnp.arange(a, dtype=jnp.int32) - first[e_sorted])
    row_token = jnp.zeros((n_tiles * bm,), jnp.int32).at[pos_sorted].set(order // 2)
    pos = jnp.zeros((a,), jnp.int32).at[order].set(pos_sorted)
    tile_expert = jnp.minimum(jnp.searchsorted(tile_end, jnp.arange(n_tiles, dtype=jnp.int32), side="right"),
                              n_exp - 1).astype(jnp.int32)
    return dict(row_token=row_token, pos=pos, tile_expert=tile_expert, n_used=tile_end[-1:], n_tiles=n_tiles)


def _moe_gather_body(tok_ref, x_hbm, o_ref, buf, sem, *, bm):
    base = pl.program_id(0) * bm

    def slab_copy(tok, r):
        return pltpu.make_async_copy(x_hbm.at[tok], buf.at[r], sem)

    def issue(r, carry):
        slab_copy(tok_ref[base + r], r).start()
        return carry

    def drain(r, carry):
        slab_copy(0, r).wait()
        return carry

    lax.fori_loop(0, bm, issue, 0, unroll=8)
    lax.fori_loop(0, bm, drain, 0, unroll=8)
    for s in range(buf.shape[1]):
        o_ref[:, s * LANES:(s + 1) * LANES] = buf[:, s, :].astype(o_ref.dtype)


def _moe_gather(row_token, x_all):
    n_slab = x_all.shape[1]
    d = n_slab * LANES
    rows = row_token.shape[0]
    bm = _tile(rows, 256, SUBLANES * 2)
    grid_spec = pltpu.PrefetchScalarGridSpec(
        num_scalar_prefetch=1,
        grid=(rows // bm,),
        in_specs=[pl.BlockSpec(memory_space=pl.ANY)],
        out_specs=pl.BlockSpec((bm, d), lambda i, tok: (i, 0)),
        scratch_shapes=[pltpu.VMEM((bm, n_slab, LANES), F32), pltpu.SemaphoreType.DMA],
    )
    return pl.pallas_call(
        functools.partial(_moe_gather_body, bm=bm),
        grid_spec=grid_spec,
        out_shape=jax.ShapeDtypeStruct((rows, d), BF16),
        compiler_params=_params("arbitrary"),
        name="moe_gather",
    )(row_token, x_all)


def _moe_down(meta, h, w_down4, layer, bm):
    rows = h.shape[0]
    d = w_down4.shape[-1]
    bn = _tile(d, 1024, LANES)
    return _ws_matmul(
        h, w_down4, layer, (0,), bn, d // bn, d // bn, bm, _epi_slabs, (), (),
        [jax.ShapeDtypeStruct((rows, d // LANES, LANES), F32)],
        [pl.BlockSpec((bm, bn // LANES, LANES), lambda j, i, te, nu: (i, j, 0))], "moe_down", meta)[0]


def _moe_combine_body(pos_ref, y_hbm, w_ref, res_ref, g_ref, b_ref, o_ref, ob_ref, buf, sem, *, bm, br, alpha):
    base = pl.program_id(0) * bm
    n_slab = buf.shape[2]

    def slab_copy(k, row, r):
        return pltpu.make_async_copy(y_hbm.at[row], buf.at[k, r], sem)

    def issue(r, carry):
        for k in range(2):
            slab_copy(k, pos_ref[2 * (base + r) + k], r).start()
        return carry

    def drain(r, carry):
        for k in range(2):
            slab_copy(k, 0, r).wait()
        return carry

    lax.fori_loop(0, bm, issue, 0, unroll=8)
    lax.fori_loop(0, bm, drain, 0, unroll=8)

    def rows(r, carry):
        sl = pl.ds(pl.multiple_of(r * br, br), br)
        y0 = jnp.concatenate([buf[0, sl, s, :] for s in range(n_slab)], axis=1)
        y1 = jnp.concatenate([buf[1, sl, s, :] for s in range(n_slab)], axis=1)
        f = w_ref[sl, 0:1] * y0 + w_ref[sl, 1:2] * y1
        y = _layer_norm(alpha * res_ref[sl, :] + f, g_ref[...], b_ref[...])
        o_ref[sl, :] = y
        ob_ref[sl, :] = y.astype(ob_ref.dtype)
        return carry

    lax.fori_loop(0, bm // br, rows, 0)


def _moe_combine(pos, y, route, res, g, b, alpha):
    m, d = res.shape
    bm = _tile(m, 128, SUBLANES * 2)
    br = _tile(bm, 32, SUBLANES * 2)
    grid_spec = pltpu.PrefetchScalarGridSpec(
        num_scalar_prefetch=1,
        grid=(m // bm,),
        in_specs=[
            pl.BlockSpec(memory_space=pl.ANY),
            pl.BlockSpec((bm, LANES), lambda i, p: (i, 0)),
            pl.BlockSpec((bm, d), lambda i, p: (i, 0)),
            pl.BlockSpec((1, d), lambda i, p: (0, 0)),
            pl.BlockSpec((1, d), lambda i, p: (0, 0)),
        ],
        out_specs=[
            pl.BlockSpec((bm, d), lambda i, p: (i, 0)),
            pl.BlockSpec((bm, d), lambda i, p: (i, 0)),
        ],
        scratch_shapes=[pltpu.VMEM((2, bm, d // LANES, LANES), F32), pltpu.SemaphoreType.DMA],
    )
    return pl.pallas_call(
        functools.partial(_moe_combine_body, bm=bm, br=br, alpha=alpha),
        grid_spec=grid_spec,
        out_shape=[jax.ShapeDtypeStruct((m, d), F32), jax.ShapeDtypeStruct((m, d), BF16)],
        compiler_params=_params("arbitrary"),
        name="moe_combine",
    )(pos, y, route, res, g.reshape(1, d), b.reshape(1, d))


def _moe_layer(streams, layer, raw, w, ln_g, ln_b, alpha):
    n_exp = raw["moe_w_router"].shape[-1]
    routes = [_router(s["x"], raw["moe_w_router"][layer], raw["moe_b_router"][layer]) for s in streams]
    route_all = jnp.concatenate(routes, axis=0)
    x_all = jnp.concatenate([s["x"] for s in streams], axis=0)
    x_all = x_all.reshape(x_all.shape[0], -1, LANES)
    bm = MOE_ROW_TILE
    meta = _route_meta(route_all[:, 2:4].astype(jnp.int32), n_exp, bm)
    xs = _moe_gather(meta["row_token"], x_all)
    h = _swiglu_ws(xs, raw["moe_w_gu"], layer, bm, meta)
    y = _moe_down(meta, h, raw["moe_w_down"], layer, bm)
    off = 0
    for s, route in zip(streams, routes):
        pos = meta["pos"][2 * off:2 * (off + s["m"])]
        s["x"], s["xb"] = _moe_combine(pos, y, route, s["x"], ln_g, ln_b, alpha)
        off += s["m"]


def _ple_body(xb_ref, x_ref, p_ref, wp_ref, wg_ref, o_ref, ob_ref):
    gate = _sigmoid(_dot(xb_ref[...], wg_ref[...]))
    y = x_ref[...] + _dot(p_ref[...], wp_ref[...]) * gate
    o_ref[...] = y
    ob_ref[...] = y.astype(ob_ref.dtype)


def _ple(x, xb, pb, w_proj, w_gate):
    m, d = x.shape
    dp = pb.shape[-1]
    bm = _tile(m, 1024, SUBLANES * 2)
    bn = _tile(d, 512, LANES)
    return pl.pallas_call(
        _ple_body,
        grid=(m // bm, d // bn),
        in_specs=[
            pl.BlockSpec((bm, d), lambda i, j: (i, 0)),
            pl.BlockSpec((bm, bn), lambda i, j: (i, j)),
            pl.BlockSpec((bm, dp), lambda i, j: (i, 0)),
            pl.BlockSpec((dp, bn), lambda i, j: (0, j)),
            pl.BlockSpec((d, bn), lambda i, j: (0, j)),
        ],
        out_specs=[
            pl.BlockSpec((bm, bn), lambda i, j: (i, j)),
            pl.BlockSpec((bm, bn), lambda i, j: (i, j)),
        ],
        out_shape=[jax.ShapeDtypeStruct((m, d), F32), jax.ShapeDtypeStruct((m, d), BF16)],
        compiler_params=_params("parallel", "arbitrary"),
        name="ple",
    )(xb, x, pb, w_proj, w_gate)


def _heads_body(x_ref, w_ref, cos_ref, sin_ref, o_ref, or_ref, *, heads_per_blk, rope_blocks, two_out):
    acc = _dot(x_ref[...], w_ref[...])
    cos = cos_ref[...]
    sin = sin_ref[...]

    def roped():
        return [_rope_head(acc[:, h * LANES:(h + 1) * LANES], cos, sin) for h in range(heads_per_blk)]

    if two_out:
        o_ref[...] = acc.astype(o_ref.dtype)
        for h, y in enumerate(roped()):
            or_ref[:, h * LANES:(h + 1) * LANES] = y.astype(or_ref.dtype)
    else:
        j = pl.program_id(1)
        is_rope = functools.reduce(jnp.logical_or, [j == rb for rb in rope_blocks])

        @pl.when(is_rope)
        def _():
            for h, y in enumerate(roped()):
                o_ref[:, h * LANES:(h + 1) * LANES] = y
                or_ref[:, h * LANES:(h + 1) * LANES] = y.astype(or_ref.dtype)

        @pl.when(jnp.logical_not(is_rope))
        def _():
            o_ref[...] = acc
            or_ref[...] = acc.astype(or_ref.dtype)


def _heads_proj(xb, w, cos, sin, *, two_out, rope_blocks, out_dtypes):
    m, d = xb.shape
    n = w.shape[-1]
    bn = 4 * LANES
    bm = _tile(m, 1024, SUBLANES * 2)
    return pl.pallas_call(
        functools.partial(_heads_body, heads_per_blk=bn // LANES, rope_blocks=rope_blocks, two_out=two_out),
        grid=(m // bm, n // bn),
        in_specs=[
            pl.BlockSpec((bm, d), lambda i, j: (i, 0)),
            pl.BlockSpec((d, bn), lambda i, j: (0, j)),
            pl.BlockSpec((bm, LANES), lambda i, j: (i, 0)),
            pl.BlockSpec((bm, LANES), lambda i, j: (i, 0)),
        ],
        out_specs=[
            pl.BlockSpec((bm, bn), lambda i, j: (i, j)),
            pl.BlockSpec((bm, bn), lambda i, j: (i, j)),
        ],
        out_shape=[jax.ShapeDtypeStruct((m, n), out_dtypes[0]), jax.ShapeDtypeStruct((m, n), out_dtypes[1])],
        compiler_params=_params("parallel", "arbitrary"),
        name="heads_proj",
    )(xb, w, cos, sin)


def _gates_body(x_ref, w_ref, o_ref):
    o_ref[...] = _sigmoid(_dot(x_ref[...], w_ref[...]))


def _gates_proj(xb, w):
    m, d = xb.shape
    n = w.shape[-1]
    bm = _tile(m, 1024, SUBLANES * 2)
    return pl.pallas_call(
        _gates_body,
        grid=(m // bm,),
        in_specs=[pl.BlockSpec((bm, d), lambda i: (i, 0)), pl.BlockSpec((d, n), lambda i: (0, 0))],
        out_specs=pl.BlockSpec((bm, n), lambda i: (i, 0)),
        out_shape=jax.ShapeDtypeStruct((m, n), F32),
        compiler_params=_params("parallel"),
        name="gates_proj",
    )(xb, w)


def _gather_body(pt_ref, x_ref, oc_ref, os_ref):
    n_parts, n_kv = x_ref.shape[1], x_ref.shape[2]
    for part in range(n_parts):
        for g in range(n_kv):
            col = (part % (n_parts // 2)) * n_kv + g
            rows = x_ref[:, part, g, :]
            if part < n_parts // 2:
                oc_ref[:, col * LANES:(col + 1) * LANES] = rows
            else:
                os_ref[:, col * LANES:(col + 1) * LANES] = rows.astype(os_ref.dtype)


def _gather_pages(cache_kv, page_table):
    n_pool, page, n_parts, n_kv, hd = cache_kv.shape
    nb, n_pages = page_table.shape
    half = n_parts * n_kv * hd // 2
    grid_spec = pltpu.PrefetchScalarGridSpec(
        num_scalar_prefetch=1,
        grid=(nb, n_pages),
        in_specs=[pl.BlockSpec((None, page, n_parts, n_kv, hd), lambda b, p, pt: (pt[b, p], 0, 0, 0, 0))],
        out_specs=[
            pl.BlockSpec((None, page, half), lambda b, p, pt: (b, p, 0)),
            pl.BlockSpec((None, page, half), lambda b, p, pt: (b, p, 0)),
        ],
    )
    return pl.pallas_call(
        _gather_body,
        grid_spec=grid_spec,
        out_shape=[
            jax.ShapeDtypeStruct((nb, n_pages * page, half), F32),
            jax.ShapeDtypeStruct((nb, n_pages * page, half), BF16),
        ],
        compiler_params=_params("parallel", "arbitrary"),
        name="gather_pages",
    )(page_table, cache_kv)


def _compress_body(x_ref, pe_ref, w_ref, o_ref, *, nc):
    half = nc // 2
    acc = jnp.zeros((nc, LANES), F32)
    for l in range(CMP_BLOCK):
        xe = x_ref[pl.ds(l, half, stride=2 * CMP_BLOCK), :]
        xo = x_ref[pl.ds(l + CMP_BLOCK, half, stride=2 * CMP_BLOCK), :]
        xl = jnp.concatenate([xe, xo], axis=0) + pe_ref[l:l + 1, :]
        acc = acc + _dot(xl.astype(BF16), w_ref[l])
    o_ref[...] = acc.astype(o_ref.dtype)


def _compress(x3, pe, w, n_kv, nc):
    nb = x3.shape[0]
    tc = nc * CMP_BLOCK
    return pl.pallas_call(
        functools.partial(_compress_body, nc=nc),
        grid=(nb, 2, n_kv),
        in_specs=[
            pl.BlockSpec((None, tc, LANES), lambda b, c, g: (b, 0, c * n_kv + g)),
            pl.BlockSpec((None, CMP_BLOCK, LANES), lambda b, c, g: (c, 0, 0)),
            pl.BlockSpec((None, CMP_BLOCK, LANES, LANES), lambda b, c, g: (c, 0, 0, 0)),
        ],
        out_specs=pl.BlockSpec((None, None, None, nc, LANES), lambda b, c, g: (c, b, g, 0, 0)),
        out_shape=jax.ShapeDtypeStruct((2, nb, n_kv, nc, LANES), BF16),
        compiler_params=_params("parallel", "parallel", "arbitrary"),
        name="compress",
    )(x3, pe, w)


def _cmp_block_of_lane(nc, shape):
    lane = lax.broadcasted_iota(jnp.int32, shape, len(shape) - 1)
    half = nc // 2
    return jnp.where(lane < half, 2 * lane, 2 * (lane - half) + 1)


def _cmp_branch(q, kc, vc, trow, nc, scale):
    s = _dot_nt(q, kc) * scale
    n = _cmp_block_of_lane(nc, s.shape)
    ok = ((n + 1) * CMP_BLOCK - 1) <= trow
    sm = jnp.where(ok, s, NEG)
    e = jnp.exp(sm - jnp.max(sm, axis=-1, keepdims=True))
    p = jnp.where(ok, e / jnp.sum(e, axis=-1, keepdims=True), 0.0)
    return p, _dot(p.astype(BF16), vc)


def _online_update(carry, s, ok, v):
    m, l, acc = carry
    sm = jnp.where(ok, s, NEG)
    m_new = jnp.maximum(m, jnp.max(sm, axis=-1, keepdims=True))
    a = jnp.exp(m - m_new)
    p = jnp.where(ok, jnp.exp(sm - m_new), 0.0)
    return m_new, a * l + jnp.sum(p, axis=-1, keepdims=True), a * acc + _dot(p.astype(BF16), v)


def _online_init(rows):
    return (jnp.full((rows, 1), NEG, F32), jnp.zeros((rows, 1), F32), jnp.zeros((rows, LANES), F32))


def _block_mask(sel_rows, k0, tk):
    nbk = sel_rows.shape[-1]
    blk = lax.broadcasted_iota(jnp.int32, (nbk, tk), 0)
    kpos = k0 + lax.broadcasted_iota(jnp.int32, (nbk, tk), 1)
    expand = (blk == kpos // SEL_BLOCK).astype(BF16)
    return _dot(sel_rows, expand)


def _attn_prompt_body(q_ref, qr_ref, gt_ref, kc_ref, vc_ref, ks_ref, vs_ref, kw_ref, vw_ref, o_ref,
                      *, n_rep, tq, seq, nc, nsb, scale):
    t0 = pl.program_id(2) * tq
    rows = n_rep * tq
    q = jnp.concatenate([q_ref[:, r * LANES:(r + 1) * LANES] for r in range(n_rep)], axis=0)
    qr = jnp.concatenate([qr_ref[:, r * LANES:(r + 1) * LANES] for r in range(n_rep)], axis=0)
    trow = t0 + lax.rem(lax.broadcasted_iota(jnp.int32, (rows, 1), 0), tq)

    p_cmp, o_cmp = _cmp_branch(q, kc_ref[...], vc_ref[...], trow, nc, scale)
    imp = p_cmp[0:tq]
    for r in range(1, n_rep):
        imp = imp + p_cmp[r * tq:(r + 1) * tq]
    half = nc // 2
    imp = imp[:, :half] + imp[:, half:nc]
    imp = jnp.concatenate([imp, jnp.zeros((tq, LANES - half), F32)], axis=1)
    imp_t = imp.T[:nsb]

    j = lax.broadcasted_iota(jnp.int32, (nsb, tq), 0)
    tcol = t0 + lax.broadcasted_iota(jnp.int32, (nsb, tq), 1)
    cur = tcol // SEL_BLOCK
    forced = (j == 0) | (j == cur) | (j == cur - 1)
    avail = j * SEL_BLOCK <= tcol
    work = jnp.where(avail, imp_t + FORCE * forced.astype(F32), -1.0)
    sel = jnp.zeros((nsb, tq), F32)
    for _ in range(min(N_SEL, nsb)):
        mx = jnp.max(work, axis=0, keepdims=True)
        idx = jnp.min(jnp.where(work == mx, j, nsb), axis=0, keepdims=True)
        pick = j == idx
        sel = jnp.where(pick, 1.0, sel)
        work = jnp.where(pick, -jnp.inf, work)
    sel = jnp.where(avail, sel, 0.0)
    sel = jnp.concatenate([sel, jnp.zeros((LANES - nsb, tq), F32)], axis=0).T
    sel_rows = jnp.concatenate([sel.astype(BF16)] * n_rep, axis=0)

    tk = min(512, seq)
    n_tiles = (t0 + tq + tk - 1) // tk

    def sel_step(kt, carry):
        k0 = pl.multiple_of(kt * tk, tk)
        s = _dot_nt(qr, ks_ref[pl.ds(k0, tk), :]) * scale
        kpos = k0 + lax.broadcasted_iota(jnp.int32, (rows, tk), 1)
        ok = (_block_mask(sel_rows, k0, tk) > 0.5) & (kpos <= trow)
        return _online_update(carry, s, ok, vs_ref[pl.ds(k0, tk), :])

    _, l_sel, acc_sel = lax.fori_loop(0, n_tiles, sel_step, _online_init(rows))
    o_sel = acc_sel / l_sel

    wl = min(WINDOW + tq, seq)
    w0 = pl.multiple_of(jnp.maximum(t0 + tq - wl, 0), tq)
    s = _dot_nt(qr, kw_ref[pl.ds(w0, wl), :]) * scale
    dist = trow - (w0 + lax.broadcasted_iota(jnp.int32, (rows, wl), 1))
    ok = (dist >= 0) & (dist < WINDOW)
    _, l_win, acc_win = _online_update(_online_init(rows), s, ok, vw_ref[pl.ds(w0, wl), :])
    o_win = acc_win / l_win

    for r in range(n_rep):
        sl = slice(r * tq, (r + 1) * tq)
        o = (gt_ref[:, r:r + 1] * o_cmp[sl]
             + gt_ref[:, n_rep + r:n_rep + r + 1] * o_sel[sl]
             + gt_ref[:, 2 * n_rep + r:2 * n_rep + r + 1] * o_win[sl])
        o_ref[:, r * LANES:(r + 1) * LANES] = o.astype(o_ref.dtype)


def _attn_prompt(q, qr, gates_g, cmp_kv, kvb, nb, seq, n_kv):
    m, d = q.shape
    n_rep = d // (n_kv * LANES)
    nc = seq // CMP_BLOCK
    nsb = -(-seq // SEL_BLOCK)
    tq = _tile(seq, 128, LANES)
    nq = seq // tq
    hw = n_rep * LANES
    kv_spec = lambda part: pl.BlockSpec((seq, LANES), lambda b, g, i: (b, part * n_kv + g))
    cmp_spec = lambda c: pl.BlockSpec((None, None, None, nc, LANES), lambda b, g, i: (c, b, g, 0, 0))
    return pl.pallas_call(
        functools.partial(_attn_prompt_body, n_rep=n_rep, tq=tq, seq=seq, nc=nc, nsb=nsb,
                          scale=float(LANES) ** -0.5),
        grid=(nb, n_kv, nq),
        in_specs=[
            pl.BlockSpec((tq, hw), lambda b, g, i: (b * nq + i, g)),
            pl.BlockSpec((tq, hw), lambda b, g, i: (b * nq + i, g)),
            pl.BlockSpec((None, tq, 3 * n_rep), lambda b, g, i: (g, b * nq + i, 0)),
            cmp_spec(0), cmp_spec(1),
            kv_spec(2), kv_spec(3), kv_spec(4), kv_spec(5),
        ],
        out_specs=pl.BlockSpec((tq, hw), lambda b, g, i: (b * nq + i, g)),
        out_shape=jax.ShapeDtypeStruct((m, d), BF16),
        compiler_params=_params("parallel", "parallel", "arbitrary"),
        name="attn_prompt",
    )(q, qr, gates_g, cmp_kv, cmp_kv, kvb, kvb, kvb, kvb)


def _pad_rows(a, rows):
    return jnp.concatenate([a, jnp.zeros((rows - a.shape[0], a.shape[1]), a.dtype)], axis=0)


def _attn_sample_body(q_ref, qr_ref, gt_ref, kc_ref, vc_ref, ksp_ref, vsp_ref, ksn_ref, vsn_ref,
                      kwp_ref, vwp_ref, kwn_ref, vwn_ref, o_ref,
                      *, n_rep, tn, past, nc, nsb, nbk, scale):
    rows = n_rep * tn
    q = jnp.concatenate([q_ref[:, r * LANES:(r + 1) * LANES] for r in range(n_rep)], axis=0).astype(BF16)
    qr = jnp.concatenate([qr_ref[:, r * LANES:(r + 1) * LANES] for r in range(n_rep)], axis=0).astype(BF16)
    trow = past + lax.rem(lax.broadcasted_iota(jnp.int32, (rows, 1), 0), tn)

    p_cmp, o_cmp = _cmp_branch(q, kc_ref[...], vc_ref[...], trow, nc, scale)
    imp = p_cmp[0:tn]
    for r in range(1, n_rep):
        imp = imp + p_cmp[r * tn:(r + 1) * tn]
    half = nc // 2
    imp = imp[:, :half] + imp[:, half:nc]
    imp = jnp.concatenate([imp, jnp.zeros((tn, nbk - half), F32)], axis=1)

    j = lax.broadcasted_iota(jnp.int32, (tn, nbk), 1)
    tcol = past + lax.broadcasted_iota(jnp.int32, (tn, nbk), 0)
    cur = tcol // SEL_BLOCK
    forced = (j == 0) | (j == cur) | (j == cur - 1)
    avail = (j * SEL_BLOCK <= tcol) & (j < nsb)
    work = jnp.where(avail, imp + FORCE * forced.astype(F32), -1.0)
    work = jnp.where(j < nsb, work, -jnp.inf)
    sel = jnp.zeros((tn, nbk), F32)
    for _ in range(min(N_SEL, nsb)):
        mx = jnp.max(work, axis=1, keepdims=True)
        idx = jnp.min(jnp.where(work == mx, j, nbk), axis=1, keepdims=True)
        pick = j == idx
        sel = jnp.where(pick, 1.0, sel)
        work = jnp.where(pick, -jnp.inf, work)
    sel = jnp.where(avail, sel, 0.0)
    sel_rows = jnp.concatenate([sel.astype(BF16)] * n_rep, axis=0)

    tk = min(1024, past)

    def sel_step(kt, carry):
        k0 = pl.multiple_of(kt * tk, tk)
        s = _dot_nt(qr, ksp_ref[pl.ds(k0, tk), :]) * scale
        kpos = k0 + lax.broadcasted_iota(jnp.int32, (rows, tk), 1)
        ok = (_block_mask(sel_rows, k0, tk) > 0.5) & (kpos <= trow)
        return _online_update(carry, s, ok, vsp_ref[pl.ds(k0, tk), :])

    carry = lax.fori_loop(0, past // tk, sel_step, _online_init(rows))
    kn = _pad_rows(ksn_ref[...], LANES).astype(BF16)
    vn = _pad_rows(vsn_ref[...], LANES).astype(BF16)
    kpos = past + lax.broadcasted_iota(jnp.int32, (rows, LANES), 1)
    ok = (_block_mask(sel_rows, past, LANES) > 0.5) & (kpos <= trow)
    _, l_sel, acc_sel = _online_update(carry, _dot_nt(qr, kn) * scale, ok, vn)
    o_sel = acc_sel / l_sel

    wb = kwp_ref.shape[0]
    dist = trow - (past - wb + lax.broadcasted_iota(jnp.int32, (rows, wb), 1))
    ok = (dist >= 0) & (dist < WINDOW)
    carry = _online_update(_online_init(rows), _dot_nt(qr, kwp_ref[...].astype(BF16)) * scale, ok,
                           vwp_ref[...].astype(BF16))
    kn = _pad_rows(kwn_ref[...], LANES).astype(BF16)
    vn = _pad_rows(vwn_ref[...], LANES).astype(BF16)
    dist = trow - kpos
    ok = (dist >= 0) & (dist < WINDOW)
    _, l_win, acc_win = _online_update(carry, _dot_nt(qr, kn) * scale, ok, vn)
    o_win = acc_win / l_win

    for r in range(n_rep):
        sl = slice(r * tn, (r + 1) * tn)
        o = (gt_ref[:, r:r + 1] * o_cmp[sl]
             + gt_ref[:, n_rep + r:n_rep + r + 1] * o_sel[sl]
             + gt_ref[:, 2 * n_rep + r:2 * n_rep + r + 1] * o_win[sl])
        o_ref[:, r * LANES:(r + 1) * LANES] = o


def _attn_sample(q, qr, gates_g, cmp_kv, sel_past, kv_new, win_past, nb, tn, n_kv):
    m, d = q.shape
    n_rep = d // (n_kv * LANES)
    past = sel_past.shape[1]
    nc = cmp_kv.shape[3]
    nsb = -(-(past + tn) // SEL_BLOCK)
    nbk = -(-nsb // LANES) * LANES
    wb = win_past.shape[1]
    hw = n_rep * LANES
    new_spec = lambda part: pl.BlockSpec((tn, LANES), lambda b, g: (b, part * n_kv + g))
    cmp_spec = lambda c: pl.BlockSpec((None, None, None, nc, LANES), lambda b, g: (c, b, g, 0, 0))
    return pl.pallas_call(
        functools.partial(_attn_sample_body, n_rep=n_rep, tn=tn, past=past, nc=nc, nsb=nsb, nbk=nbk,
                          scale=float(LANES) ** -0.5),
        grid=(nb, n_kv),
        in_specs=[
            pl.BlockSpec((tn, hw), lambda b, g: (b, g)),
            pl.BlockSpec((tn, hw), lambda b, g: (b, g)),
            pl.BlockSpec((None, tn, 3 * n_rep), lambda b, g: (g, b, 0)),
            cmp_spec(0), cmp_spec(1),
            pl.BlockSpec((None, past, LANES), lambda b, g: (b, 0, g)),
            pl.BlockSpec((None, past, LANES), lambda b, g: (b, 0, n_kv + g)),
            new_spec(2), new_spec(3),
            pl.BlockSpec((None, wb, LANES), lambda b, g: (b, 0, g)),
            pl.BlockSpec((None, wb, LANES), lambda b, g: (b, 0, n_kv + g)),
            new_spec(4), new_spec(5),
        ],
        out_specs=pl.BlockSpec((tn, hw), lambda b, g: (b, g)),
        out_shape=jax.ShapeDtypeStruct((m, d), F32),
        compiler_params=_params("parallel", "arbitrary"),
        name="attn_sample",
    )(q, qr, gates_g, cmp_kv, cmp_kv, sel_past, sel_past, kv_new, kv_new, win_past, win_past, kv_new, kv_new)


def _rope_tables(pos):
    half = LANES // 2
    inv = ROPE_THETA ** (-jnp.arange(half, dtype=F32) / half)
    ang = pos.astype(F32)[:, None] * inv[None, :]
    cos, sin = jnp.cos(ang), jnp.sin(ang)
    return jnp.concatenate([cos, cos], axis=-1), jnp.concatenate([-sin, sin], axis=-1)


def _prep_weights(a_w_in, w_kv, cmp_pe, cmp_w, b_w_qg, ffn_w_down, ple_w_proj, ple_w_gate, n_kv):
    d = a_w_in.shape[1]
    n_heads = d // LANES
    n_rep = n_heads // n_kv
    f = ffn_w_down.shape[1]
    fp = -(-f // 1024) * 1024
    w = {}
    w["a_w_in"] = a_w_in.astype(BF16)
    w["w_kv"] = w_kv.astype(BF16)
    w["cmp_pe"] = jnp.transpose(cmp_pe, (1, 0, 2))
    w["cmp_w"] = jnp.transpose(cmp_w, (1, 0, 2, 3)).astype(BF16)
    w["w_q"] = b_w_qg[:, :, :d].astype(BF16)
    wg = b_w_qg[:, :, d:].reshape(-1, d, n_kv, n_rep, 3)
    wg = jnp.transpose(wg, (0, 1, 2, 4, 3)).reshape(-1, d, 3 * n_heads)
    w["w_gates"] = jnp.pad(wg, ((0, 0), (0, 0), (0, LANES - 3 * n_heads))).astype(BF16)
    w["ffn_w_down"] = jnp.pad(ffn_w_down, ((0, 0), (0, fp - f), (0, 0))).astype(BF16)
    w["ffn_f"] = fp
    w["ple_w_proj"] = ple_w_proj.astype(BF16)
    w["ple_w_gate"] = ple_w_gate.astype(BF16)
    return w


def _new_stream(x3, p4, conv_prev, past_ctx):
    nb, seq, d = x3.shape
    m = nb * seq
    past = 0 if past_ctx is None else past_ctx["past"]
    cos1, sin1 = _rope_tables(past + jnp.arange(seq, dtype=jnp.int32))
    x = x3.reshape(m, d)
    return dict(nb=nb, seq=seq, m=m, x=x, xb=x.astype(BF16), pb=p4.reshape(p4.shape[0], m, -1).astype(BF16),
                conv_prev=conv_prev, past_ctx=past_ctx, cos=jnp.tile(cos1, (nb, 1)), sin=jnp.tile(sin1, (nb, 1)),
                conv_states=[], ctx=None)


def _conv_mixer(s, i, w, raw, ln_g, ln_b, alpha):
    d = s["x"].shape[1]
    prev = jnp.zeros((s["nb"], CONV_W - 1, d), F32) if s["conv_prev"] is None else s["conv_prev"][i]
    v, st = _conv_in(s["xb"], w["a_w_in"][i], raw["a_conv_w"][i], prev, s["seq"])
    s["conv_states"].append(st[:, SUBLANES - (CONV_W - 1):])
    s["x"], s["xb"] = _proj_ln_ws(v, raw["a_w_out"][:, None], i, s["x"], ln_g, ln_b, alpha)


def _nsa_mixer(s, li, w, raw, ln_g, ln_b, alpha, n_kv):
    m, d = s["x"].shape
    n_heads = d // LANES
    n_rep = n_heads // n_kv
    prompt = s["past_ctx"] is None
    qdt = BF16 if prompt else F32
    q, qr = _heads_proj(s["xb"], w["w_q"][li], s["cos"], s["sin"], two_out=True, rope_blocks=(),
                        out_dtypes=(qdt, qdt))
    gates = _gates_proj(s["xb"], w["w_gates"][li])[:, :3 * n_heads]
    gates_g = jnp.transpose(gates.reshape(m, n_kv, 3 * n_rep), (1, 0, 2))
    ctx = s["ctx"]
    if prompt:
        o = _attn_prompt(q, qr, gates_g, ctx["cmp_kv"], ctx["kvb"], s["nb"], s["seq"], n_kv)
    else:
        o = _attn_sample(q, qr, gates_g, ctx["cmp_kv"], s["past_ctx"]["sel_past"], ctx["kv_all"],
                         s["past_ctx"]["win_past"], s["nb"], s["seq"], n_kv).astype(BF16)
    s["x"], s["xb"] = _proj_ln_ws(o, raw["b_w_o"][:, None], li, s["x"], ln_g, ln_b, alpha)


def _shared_context(s, w, n_kv):
    kv_all, kvb = _heads_proj(s["xb"], w["w_kv"], s["cos"], s["sin"], two_out=False, rope_blocks=(2, 4),
                              out_dtypes=(F32, BF16))
    if s["past_ctx"] is None:
        cmp_kv = _compress(kv_all.reshape(s["nb"], s["seq"], -1), w["cmp_pe"], w["cmp_w"], n_kv,
                           s["seq"] // CMP_BLOCK)
    else:
        past = s["past_ctx"]["past"]
        assert past % CMP_BLOCK == 0 and s["seq"] < CMP_BLOCK
        cmp_kv = _compress(s["past_ctx"]["cmp_past"], w["cmp_pe"], w["cmp_w"], n_kv, past // CMP_BLOCK)
    s["ctx"] = dict(kv_all=kv_all, kvb=kvb, cmp_kv=cmp_kv)


def _trunk(streams, w, raw, n_kv):
    depth = raw["ln_g"].shape[0]
    n_a = depth // 2
    alpha = float((2 * depth) ** 0.25)
    for i in range(depth):
        ln_g, ln_b = raw["ln_g"][i], raw["ln_b"][i]
        for s in streams:
            if i < n_a:
                _conv_mixer(s, i, w, raw, ln_g[0], ln_b[0], alpha)
            else:
                _nsa_mixer(s, i - n_a, w, raw, ln_g[0], ln_b[0], alpha, n_kv)
        if i % 2 == 0:
            for s in streams:
                h = _swiglu_ws(s["xb"], raw["ffn_w_gu"][:, None], i // 2, _tile(s["m"], 1024, SUBLANES * 2),
                               f_pad=w["ffn_f"])
                s["x"], s["xb"] = _proj_ln(h, w["ffn_w_down"][i // 2], s["x"], ln_g[1], ln_b[1], alpha)
        else:
            _moe_layer(streams, i // 2, raw, w, ln_g[1], ln_b[1], alpha)
        for s in streams:
            s["x"], s["xb"] = _ple(s["x"], s["xb"], s["pb"][i], w["ple_w_proj"][i], w["ple_w_gate"][i])
            if i == n_a - 1:
                _shared_context(s, w, n_kv)
    outs = []
    for s in streams:
        nb, seq, hd_blk = s["nb"], s["seq"], n_kv * LANES
        kv_all = s["ctx"]["kv_all"]
        rows = kv_all[:, :4 * hd_blk].reshape(nb, seq, 4, n_kv, LANES)
        win = kv_all[:, 4 * hd_blk:].reshape(nb, seq, 2, n_kv, LANES)
        outs.append((s["x"].reshape(nb, seq, -1), rows, win, jnp.stack(s["conv_states"])))
    return outs


def kernel(x_prompt, x_sample, cache_kv, cache_win, state_conv, page_table, p_prompt, p_sample,
           a_w_in, a_conv_w, a_w_out, w_kv, cmp_pe, cmp_w, b_w_qg, b_w_o, ffn_w_gu, ffn_w_down,
           moe_w_router, moe_b_router, moe_w_gu, moe_w_down, ple_w_proj, ple_w_gate, ln_g, ln_b):
    n_kv = cache_kv.shape[3]
    raw = dict(a_conv_w=a_conv_w, a_w_out=a_w_out, b_w_o=b_w_o, ffn_w_gu=ffn_w_gu, moe_w_router=moe_w_router,
               moe_b_router=moe_b_router, moe_w_gu=moe_w_gu, moe_w_down=moe_w_down, ln_g=ln_g, ln_b=ln_b)
    w = _prep_weights(a_w_in, w_kv, cmp_pe, cmp_w, b_w_qg, ffn_w_down, ple_w_proj, ple_w_gate, n_kv)

    n_pool, page = cache_kv.shape[:2]
    cmp_past, sel_past = _gather_pages(cache_kv, page_table)
    nb_s, wb = cache_win.shape[:2]
    past_ctx = dict(past=page_table.shape[1] * page, cmp_past=cmp_past, sel_past=sel_past,
                    win_past=cache_win.reshape(nb_s, wb, -1))

    streams = [_new_stream(x_prompt, p_prompt, None, None),
               _new_stream(x_sample, p_sample, state_conv, past_ctx)]
    (y_p, rows_p, win_p, conv_p), (y_s, rows_s, win_s, conv_s) = _trunk(streams, w, raw, n_kv)
    win_state_p = win_p[:, -min(WINDOW, x_prompt.shape[1]):]
    win_state_s = jnp.concatenate([cache_win.astype(win_s.dtype), win_s], axis=1)[:, -wb:]

    return (y_p, y_s, rows_p, rows_s, win_state_p, win_state_s, conv_p, conv_s)
```

```python
import functools
import math

import jax
import jax.numpy as jnp
from jax import lax
from jax.experimental import pallas as pl
from jax.experimental.pallas import tpu as pltpu

F32 = jnp.float32
BF16 = jnp.bfloat16

CONV_W = 3
CMP_BLOCK = 32
SEL_BLOCK = 64
N_SEL = 16
WINDOW = 512
ROPE_THETA = 10000.0
LN_EPS = 1e-5
NEG = -1e30
FORCE = 1e4

LANES = 128
SUBLANES = 8
VMEM_LIMIT_BYTES = 56 * 1024 * 1024


def _params(*sem):
    return pltpu.CompilerParams(dimension_semantics=sem, vmem_limit_bytes=VMEM_LIMIT_BYTES)


def _tile(n, pref, align):
    if n <= pref:
        return n
    t = (pref // align) * align
    while t >= align:
        if n % t == 0:
            return t
        t -= align
    raise ValueError(f"no tile for {n} (pref {pref}, align {align})")


def _dot(a, b):
    return jnp.dot(a, b, preferred_element_type=F32)


def _dot_nt(a, b):
    return lax.dot_general(a, b, (((1,), (1,)), ((), ())), preferred_element_type=F32)


def _sigmoid(x):
    return 1.0 / (1.0 + jnp.exp(-x))


def _rope_head(a, cos, sin_signed):
    return a * cos + pltpu.roll(a, a.shape[-1] // 2, axis=1) * sin_signed


def _layer_norm(z, g, b):
    mu = jnp.mean(z, axis=-1, keepdims=True)
    zc = z - mu
    var = jnp.mean(zc * zc, axis=-1, keepdims=True)
    return zc * lax.rsqrt(var + LN_EPS) * g + b


WS_STAGE_BYTES = 4 * 1024 * 1024


def _ws_body(te_ref, nu_ref, x_ref, w_hbm, *rest, layer, seg_offsets, bn, ck, nj_valid, epilogue, n_extra, n_out):
    extra = rest[:n_extra]
    outs = rest[n_extra:n_extra + n_out]
    cache, stage, acc_ref, sem = rest[n_extra + n_out:]
    j = pl.program_id(0)
    i = pl.program_id(1)
    e = te_ref[i]
    nseg = len(seg_offsets)
    n_chunks = x_ref.shape[1] // ck
    col_ok = j < nj_valid
    valid = jnp.logical_and(i < nu_ref[0], col_ok)
    changed = jnp.logical_and(jnp.logical_or(i == 0, e != te_ref[jnp.maximum(i - 1, 0)]), col_ok)

    def copies(c, slot):
        return [pltpu.make_async_copy(
            w_hbm.at[layer, e, pl.ds(c * ck, ck), pl.ds(pl.multiple_of(off + j * bn, LANES), bn)],
            stage.at[slot, s], sem.at[slot]) for s, off in enumerate(seg_offsets)]

    @pl.when(changed)
    def _():
        for cp in copies(0, 0):
            cp.start()
        for c in range(n_chunks):
            slot = c % 2
            if c + 1 < n_chunks:
                for cp in copies(c + 1, 1 - slot):
                    cp.start()
            for cp in copies(c, slot):
                cp.wait()
            for s in range(nseg):
                cache[c * ck:(c + 1) * ck, s * bn:(s + 1) * bn] = stage[slot, s].astype(BF16)
            part = _dot(x_ref[:, c * ck:(c + 1) * ck], cache[c * ck:(c + 1) * ck, :])
            if c == 0:
                acc_ref[...] = part
            else:
                acc_ref[...] += part

    @pl.when(jnp.logical_and(valid, jnp.logical_not(changed)))
    def _():
        acc_ref[...] = _dot(x_ref[...], cache[...])

    @pl.when(valid)
    def _():
        epilogue(acc_ref, extra, outs)

    @pl.when(jnp.logical_not(valid))
    def _():
        for o in outs:
            o[...] = jnp.zeros_like(o)


def _ws_matmul(x, w4, layer, seg_offsets, bn, nj, nj_valid, bm, epilogue, extra, extra_specs, out_shape, out_specs,
               name, meta=None):
    rows, k = x.shape
    n_tiles = rows // bm
    nseg = len(seg_offsets)
    ck = _tile(k, max(LANES, WS_STAGE_BYTES // (2 * nseg * bn * 4)), LANES)
    if meta is None:
        te = jnp.zeros((n_tiles,), jnp.int32)
        nu = jnp.full((1,), n_tiles, jnp.int32)
    else:
        te, nu = meta["tile_expert"], meta["n_used"]
    grid_spec = pltpu.PrefetchScalarGridSpec(
        num_scalar_prefetch=2,
        grid=(nj, n_tiles),
        in_specs=[pl.BlockSpec((bm, k), lambda j, i, te, nu: (i, 0)), pl.BlockSpec(memory_space=pl.ANY)]
        + list(extra_specs),
        out_specs=out_specs,
        scratch_shapes=[pltpu.VMEM((k, nseg * bn), BF16), pltpu.VMEM((2, nseg, ck, bn), F32),
                        pltpu.VMEM((bm, nseg * bn), F32), pltpu.SemaphoreType.DMA((2,))],
    )
    return pl.pallas_call(
        functools.partial(_ws_body, layer=layer, seg_offsets=tuple(seg_offsets), bn=bn, ck=ck, nj_valid=nj_valid,
                          epilogue=epilogue, n_extra=len(extra), n_out=len(out_specs)),
        grid_spec=grid_spec,
        out_shape=out_shape,
        compiler_params=_params("arbitrary", "arbitrary"),
        name=name,
    )(te, nu, x, w4, *extra)


def _epi_swiglu(acc_ref, extra, outs):
    bn = outs[0].shape[-1]
    g = acc_ref[:, :bn]
    outs[0][...] = (g * _sigmoid(g) * acc_ref[:, bn:]).astype(outs[0].dtype)


def _epi_copy(acc_ref, extra, outs):
    outs[0][...] = acc_ref[...]


def _epi_ln(acc_ref, extra, outs, *, alpha, br):
    res_ref, g_ref, b_ref = extra
    o_ref, ob_ref = outs

    def rows(r, carry):
        sl = pl.ds(pl.multiple_of(r * br, br), br)
        y = _layer_norm(alpha * res_ref[sl, :] + acc_ref[sl, :], g_ref[...], b_ref[...])
        o_ref[sl, :] = y
        ob_ref[sl, :] = y.astype(ob_ref.dtype)
        return carry

    lax.fori_loop(0, o_ref.shape[0] // br, rows, 0)


def _swiglu_ws(x, w_gu4, layer, bm, meta=None, f_pad=None):
    rows = x.shape[0]
    f = w_gu4.shape[-1] // 2
    f_out = f if f_pad is None else f_pad
    bn = _tile(f, 1024, LANES) if f % 1024 == 0 else _tile(f, 256, LANES)
    return _ws_matmul(
        x, w_gu4, layer, (0, f), bn, f_out // bn, f // bn, bm, _epi_swiglu, (), (),
        [jax.ShapeDtypeStruct((rows, f_out), BF16)],
        [pl.BlockSpec((bm, bn), lambda j, i, te, nu: (i, j))], "swiglu_ws", meta)[0]


def _proj_ln_ws(lhs, w4, layer, res, g, b, alpha):
    m, d = res.shape
    bm = _tile(m, 128, SUBLANES * 2)
    br = _tile(bm, 32, SUBLANES * 2)
    row_spec = pl.BlockSpec((bm, d), lambda j, i, te, nu: (i, 0))
    vec_spec = pl.BlockSpec((1, d), lambda j, i, te, nu: (0, 0))
    return _ws_matmul(
        lhs, w4, layer, (0,), d, 1, 1, bm, functools.partial(_epi_ln, alpha=alpha, br=br),
        (res, g.reshape(1, d), b.reshape(1, d)), (row_spec, vec_spec, vec_spec),
        [jax.ShapeDtypeStruct((m, d), F32), jax.ShapeDtypeStruct((m, d), BF16)],
        [row_spec, row_spec], "proj_ln_ws")


def _conv_in_body(x_ref, wb_ref, wc_ref, wh_ref, cw_ref, prev_ref, v_ref, st_ref, conv_ref, *, seq, nseq):
    x = x_ref[...]
    b = _dot(x, wb_ref[...])
    u = _dot(x, wc_ref[...]) * _dot(x, wh_ref[...])
    w0 = cw_ref[0:1, :]
    w1 = cw_ref[1:2, :]
    w2 = cw_ref[2:3, :]
    for s in range(nseq):
        us = u[s * seq:(s + 1) * seq]
        tt = lax.broadcasted_iota(jnp.int32, us.shape, 0)
        p0 = prev_ref[s, 0:1, :]
        p1 = prev_ref[s, 1:2, :]
        u1 = jnp.where(tt >= 1, pltpu.roll(us, 1, axis=0), p1)
        u2 = jnp.where(tt >= 2, pltpu.roll(us, 2, axis=0), jnp.where(tt == 0, p0, p1))
        conv_ref[s * seq:(s + 1) * seq, :] = u2 * w0 + u1 * w1 + us * w2
        st_ref[s] = us[seq - SUBLANES:seq]
    v_ref[...] = (b * conv_ref[...]).astype(v_ref.dtype)


def _conv_in(xb, w_in, conv_w, prev, seq):
    m, d = xb.shape
    nb_seq = m // seq
    nseq = 1 if seq >= 512 else nb_seq
    bm = nseq * seq
    bn = _tile(d, 256, LANES)
    nb = d // bn
    x_mode = pl.Buffered(1) if bm * d * 2 > (8 << 20) else None
    return pl.pallas_call(
        functools.partial(_conv_in_body, seq=seq, nseq=nseq),
        grid=(m // bm, nb),
        in_specs=[
            pl.BlockSpec((bm, d), lambda i, j: (i, 0), pipeline_mode=x_mode),
            pl.BlockSpec((d, bn), lambda i, j: (0, j)),
            pl.BlockSpec((d, bn), lambda i, j: (0, j + nb)),
            pl.BlockSpec((d, bn), lambda i, j: (0, j + 2 * nb)),
            pl.BlockSpec((CONV_W, bn), lambda i, j: (0, j)),
            pl.BlockSpec((nseq, CONV_W - 1, bn), lambda i, j: (i, 0, j)),
        ],
        out_specs=[
            pl.BlockSpec((bm, bn), lambda i, j: (i, j)),
            pl.BlockSpec((nseq, SUBLANES, bn), lambda i, j: (i, 0, j)),
        ],
        out_shape=[
            jax.ShapeDtypeStruct((m, d), BF16),
            jax.ShapeDtypeStruct((nb_seq, SUBLANES, d), F32),
        ],
        scratch_shapes=[pltpu.VMEM((bm, bn), F32)],
        compiler_params=_params("parallel", "arbitrary"),
        name="conv_in",
    )(xb, w_in, w_in, w_in, conv_w, prev)


def _proj_ln_body(lhs_ref, w_ref, res_ref, g_ref, b_ref, o_ref, ob_ref, *, nk, alpha, bn, br):
    k = pl.program_id(1)
    bm, d = o_ref.shape

    @pl.when(k == 0)
    def _():
        o_ref[...] = jnp.zeros_like(o_ref)

    lhs = lhs_ref[...]
    for c in range(d // bn):
        o_ref[:, c * bn:(c + 1) * bn] += _dot(lhs, w_ref[:, c * bn:(c + 1) * bn])

    @pl.when(k == nk - 1)
    def _():
        def rows(r, carry):
            sl = pl.ds(pl.multiple_of(r * br, br), br)
            y = _layer_norm(alpha * res_ref[sl, :] + o_ref[sl, :], g_ref[...], b_ref[...])
            o_ref[sl, :] = y
            ob_ref[sl, :] = y.astype(ob_ref.dtype)
            return carry

        lax.fori_loop(0, bm // br, rows, 0)


def _proj_ln(lhs, w, res, g, b, alpha):
    m, kdim = lhs.shape
    d = w.shape[-1]
    bm = _tile(m, 512, SUBLANES * 2)
    bk = _tile(kdim, 1024, LANES)
    nk = kdim // bk
    bn = _tile(d, 512, LANES)
    br = _tile(bm, 32, SUBLANES * 2)
    return pl.pallas_call(
        functools.partial(_proj_ln_body, nk=nk, alpha=alpha, bn=bn, br=br),
        grid=(m // bm, nk),
        in_specs=[
            pl.BlockSpec((bm, bk), lambda i, k: (i, k)),
            pl.BlockSpec((bk, d), lambda i, k: (k, 0)),
            pl.BlockSpec((bm, d), lambda i, k: (i, 0), pipeline_mode=pl.Buffered(1)),
            pl.BlockSpec((1, d), lambda i, k: (0, 0)),
            pl.BlockSpec((1, d), lambda i, k: (0, 0)),
        ],
        out_specs=[
            pl.BlockSpec((bm, d), lambda i, k: (i, 0)),
            pl.BlockSpec((bm, d), lambda i, k: (i, 0)),
        ],
        out_shape=[jax.ShapeDtypeStruct((m, d), F32), jax.ShapeDtypeStruct((m, d), BF16)],
        compiler_params=_params("parallel", "arbitrary"),
        name="proj_ln",
    )(lhs, w, res, g.reshape(1, d), b.reshape(1, d))


def _router_body(x_ref, w_ref, b_ref, o_ref, *, n_experts):
    logits = jnp.dot(x_ref[...], w_ref[...], preferred_element_type=F32,
                     precision=lax.Precision.HIGHEST) + b_ref[...]
    lane = lax.broadcasted_iota(jnp.int32, logits.shape, 1)
    lg = jnp.where(lane < n_experts, logits, -jnp.inf)
    m1 = jnp.max(lg, axis=-1, keepdims=True)
    i1 = jnp.min(jnp.where(lg == m1, lane, LANES), axis=-1, keepdims=True)
    lg2 = jnp.where(lane == i1, -jnp.inf, lg)
    m2 = jnp.max(lg2, axis=-1, keepdims=True)
    i2 = jnp.min(jnp.where(lg2 == m2, lane, LANES), axis=-1, keepdims=True)
    e2 = jnp.exp(m2 - m1)
    w1 = 1.0 / (1.0 + e2)
    w2 = e2 / (1.0 + e2)
    o_ref[...] = jnp.where(lane == 0, w1, jnp.where(lane == 1, w2, jnp.where(
        lane == 2, i1.astype(F32), jnp.where(lane == 3, i2.astype(F32), 0.0))))


def _router(x, w_router, b_router):
    m, d = x.shape
    n_experts = w_router.shape[-1]
    w = jnp.pad(w_router, ((0, 0), (0, LANES - n_experts)))
    b = jnp.pad(b_router, (0, LANES - n_experts)).reshape(1, LANES)
    bm = _tile(m, 512, SUBLANES)
    return pl.pallas_call(
        functools.partial(_router_body, n_experts=n_experts),
        grid=(m // bm,),
        in_specs=[
            pl.BlockSpec((bm, d), lambda i: (i, 0)),
            pl.BlockSpec((d, LANES), lambda i: (0, 0)),
            pl.BlockSpec((1, LANES), lambda i: (0, 0)),
        ],
        out_specs=pl.BlockSpec((bm, LANES), lambda i: (i, 0)),
        out_shape=jax.ShapeDtypeStruct((m, LANES), F32),
        compiler_params=_params("parallel"),
        name="router",
    )(x, w, b)


MOE_ROW_TILE = 512


def _route_meta(eid, n_exp, bm):
    m = eid.shape[0]
    a = 2 * m
    n_tiles = -(-a // bm) + n_exp
    e_flat = eid.reshape(a)
    order = jnp.argsort(e_flat, stable=True).astype(jnp.int32)
    e_sorted = e_flat[order]
    counts = jnp.sum(e_flat[:, None] == jnp.arange(n_exp, dtype=jnp.int32)[None, :], axis=0).astype(jnp.int32)
    tiles_e = (counts + bm - 1) // bm
    tile_end = jnp.cumsum(tiles_e).astype(jnp.int32)
    tile_start = tile_end - tiles_e
    first = jnp.cumsum(counts).astype(jnp.int32) - counts
    pos_sorted = tile_start[e_sorted] * bm + (jnp.arange(a, dtype=jnp.int32) - first[e_sorted])
    row_token = jnp.zeros((n_tiles * bm,), jnp.int32).at[pos_sorted].set(order // 2)
    pos = jnp.zeros((a,), jnp.int32).at[order].set(pos_sorted)
    tile_expert = jnp.minimum(jnp.searchsorted(tile_end, jnp.arange(n_tiles, dtype=jnp.int32), side="right"),
                              n_exp - 1).astype(jnp.int32)
    return dict(row_token=row_token, pos=pos, tile_expert=tile_expert, n_used=tile_end[-1:], n_tiles=n_tiles)


def _row_copy(src_hbm, row, dst, dst_row, sem):
    return pltpu.make_async_copy(src_hbm.at[pl.ds(row, 1), :], dst.at[pl.ds(dst_row, 1), :], sem)


def _moe_gather_body(tok_ref, x_hbm, o_ref, buf, sem, *, bm):
    base = pl.program_id(0) * bm

    def issue(r, carry):
        _row_copy(x_hbm, tok_ref[base + r], buf, r, sem).start()
        return carry

    def drain(r, carry):
        _row_copy(x_hbm, 0, buf, r, sem).wait()
        return carry

    lax.fori_loop(0, bm, issue, 0, unroll=8)
    lax.fori_loop(0, bm, drain, 0, unroll=8)
    o_ref[...] = buf[...].astype(o_ref.dtype)


def _moe_gather(row_token, x_all):
    d = x_all.shape[1]
    rows = row_token.shape[0]
    bm = _tile(rows, 256, SUBLANES * 2)
    grid_spec = pltpu.PrefetchScalarGridSpec(
        num_scalar_prefetch=1,
        grid=(rows // bm,),
        in_specs=[pl.BlockSpec(memory_space=pl.ANY)],
        out_specs=pl.BlockSpec((bm, d), lambda i, tok: (i, 0)),
        scratch_shapes=[pltpu.VMEM((bm, d), F32), pltpu.SemaphoreType.DMA],
    )
    return pl.pallas_call(
        functools.partial(_moe_gather_body, bm=bm),
        grid_spec=grid_spec,
        out_shape=jax.ShapeDtypeStruct((rows, d), BF16),
        compiler_params=_params("arbitrary"),
        name="moe_gather",
    )(row_token, x_all)


def _moe_down(meta, h, w_down4, layer, bm):
    rows = h.shape[0]
    d = w_down4.shape[-1]
    bn = _tile(d, 1024, LANES)
    return _ws_matmul(
        h, w_down4, layer, (0,), bn, d // bn, d // bn, bm, _epi_copy, (), (),
        [jax.ShapeDtypeStruct((rows, d), F32)],
        [pl.BlockSpec((bm, bn), lambda j, i, te, nu: (i, j))], "moe_down", meta)[0]


def _moe_combine_body(pos_ref, y_hbm, w_ref, res_ref, g_ref, b_ref, o_ref, ob_ref, buf, sem, *, bm, br, alpha):
    base = pl.program_id(0) * bm

    def issue(r, carry):
        for k in range(2):
            _row_copy(y_hbm, pos_ref[2 * (base + r) + k], buf.at[k], r, sem).start()
        return carry

    def drain(r, carry):
        for k in range(2):
            _row_copy(y_hbm, 0, buf.at[k], r, sem).wait()
        return carry

    lax.fori_loop(0, bm, issue, 0, unroll=8)
    lax.fori_loop(0, bm, drain, 0, unroll=8)

    def rows(r, carry):
        sl = pl.ds(pl.multiple_of(r * br, br), br)
        f = w_ref[sl, 0:1] * buf[0, sl, :] + w_ref[sl, 1:2] * buf[1, sl, :]
        y = _layer_norm(alpha * res_ref[sl, :] + f, g_ref[...], b_ref[...])
        o_ref[sl, :] = y
        ob_ref[sl, :] = y.astype(ob_ref.dtype)
        return carry

    lax.fori_loop(0, bm // br, rows, 0)


def _moe_combine(pos, y, route, res, g, b, alpha):
    m, d = res.shape
    bm = _tile(m, 128, SUBLANES * 2)
    br = _tile(bm, 32, SUBLANES * 2)
    grid_spec = pltpu.PrefetchScalarGridSpec(
        num_scalar_prefetch=1,
        grid=(m // bm,),
        in_specs=[
            pl.BlockSpec(memory_space=pl.ANY),
            pl.BlockSpec((bm, LANES), lambda i, p: (i, 0)),
            pl.BlockSpec((bm, d), lambda i, p: (i, 0)),
            pl.BlockSpec((1, d), lambda i, p: (0, 0)),
            pl.BlockSpec((1, d), lambda i, p: (0, 0)),
        ],
        out_specs=[
            pl.BlockSpec((bm, d), lambda i, p: (i, 0)),
            pl.BlockSpec((bm, d), lambda i, p: (i, 0)),
        ],
        scratch_shapes=[pltpu.VMEM((2, bm, d), F32), pltpu.SemaphoreType.DMA],
    )
    return pl.pallas_call(
        functools.partial(_moe_combine_body, bm=bm, br=br, alpha=alpha),
        grid_spec=grid_spec,
        out_shape=[jax.ShapeDtypeStruct((m, d), F32), jax.ShapeDtypeStruct((m, d), BF16)],
        compiler_params=_params("arbitrary"),
        name="moe_combine",
    )(pos, y, route, res, g.reshape(1, d), b.reshape(1, d))


def _moe_layer(streams, layer, raw, w, ln_g, ln_b, alpha):
    n_exp = raw["moe_w_router"].shape[-1]
    routes = [_router(s["x"], raw["moe_w_router"][layer], raw["moe_b_router"][layer]) for s in streams]
    route_all = jnp.concatenate(routes, axis=0)
    x_all = jnp.concatenate([s["x"] for s in streams], axis=0)
    bm = MOE_ROW_TILE
    meta = _route_meta(route_all[:, 2:4].astype(jnp.int32), n_exp, bm)
    xs = _moe_gather(meta["row_token"], x_all)
    h = _swiglu_ws(xs, raw["moe_w_gu"], layer, bm, meta)
    y = _moe_down(meta, h, raw["moe_w_down"], layer, bm)
    off = 0
    for s, route in zip(streams, routes):
        pos = meta["pos"][2 * off:2 * (off + s["m"])]
        s["x"], s["xb"] = _moe_combine(pos, y, route, s["x"], ln_g, ln_b, alpha)
        off += s["m"]


def _ple_body(xb_ref, x_ref, p_ref, wp_ref, wg_ref, o_ref, ob_ref):
    gate = _sigmoid(_dot(xb_ref[...], wg_ref[...]))
    y = x_ref[...] + _dot(p_ref[...], wp_ref[...]) * gate
    o_ref[...] = y
    ob_ref[...] = y.astype(ob_ref.dtype)


def _ple(x, xb, pb, w_proj, w_gate):
    m, d = x.shape
    dp = pb.shape[-1]
    bm = _tile(m, 1024, SUBLANES * 2)
    bn = _tile(d, 512, LANES)
    return pl.pallas_call(
        _ple_body,
        grid=(m // bm, d // bn),
        in_specs=[
            pl.BlockSpec((bm, d), lambda i, j: (i, 0)),
            pl.BlockSpec((bm, bn), lambda i, j: (i, j)),
            pl.BlockSpec((bm, dp), lambda i, j: (i, 0)),
            pl.BlockSpec((dp, bn), lambda i, j: (0, j)),
            pl.BlockSpec((d, bn), lambda i, j: (0, j)),
        ],
        out_specs=[
            pl.BlockSpec((bm, bn), lambda i, j: (i, j)),
            pl.BlockSpec((bm, bn), lambda i, j: (i, j)),
        ],
        out_shape=[jax.ShapeDtypeStruct((m, d), F32), jax.ShapeDtypeStruct((m, d), BF16)],
        compiler_params=_params("parallel", "arbitrary"),
        name="ple",
    )(xb, x, pb, w_proj, w_gate)


def _heads_body(x_ref, w_ref, cos_ref, sin_ref, o_ref, or_ref, *, heads_per_blk, rope_blocks, two_out):
    acc = _dot(x_ref[...], w_ref[...])
    cos = cos_ref[...]
    sin = sin_ref[...]

    def roped():
        return [_rope_head(acc[:, h * LANES:(h + 1) * LANES], cos, sin) for h in range(heads_per_blk)]

    if two_out:
        o_ref[...] = acc.astype(o_ref.dtype)
        for h, y in enumerate(roped()):
            or_ref[:, h * LANES:(h + 1) * LANES] = y.astype(or_ref.dtype)
    else:
        j = pl.program_id(1)
        is_rope = functools.reduce(jnp.logical_or, [j == rb for rb in rope_blocks])

        @pl.when(is_rope)
        def _():
            for h, y in enumerate(roped()):
                o_ref[:, h * LANES:(h + 1) * LANES] = y
                or_ref[:, h * LANES:(h + 1) * LANES] = y.astype(or_ref.dtype)

        @pl.when(jnp.logical_not(is_rope))
        def _():
            o_ref[...] = acc
            or_ref[...] = acc.astype(or_ref.dtype)


def _heads_proj(xb, w, cos, sin, *, two_out, rope_blocks, out_dtypes):
    m, d = xb.shape
    n = w.shape[-1]
    bn = 4 * LANES
    bm = _tile(m, 1024, SUBLANES * 2)
    return pl.pallas_call(
        functools.partial(_heads_body, heads_per_blk=bn // LANES, rope_blocks=rope_blocks, two_out=two_out),
        grid=(m // bm, n // bn),
        in_specs=[
            pl.BlockSpec((bm, d), lambda i, j: (i, 0)),
            pl.BlockSpec((d, bn), lambda i, j: (0, j)),
            pl.BlockSpec((bm, LANES), lambda i, j: (i, 0)),
            pl.BlockSpec((bm, LANES), lambda i, j: (i, 0)),
        ],
        out_specs=[
            pl.BlockSpec((bm, bn), lambda i, j: (i, j)),
            pl.BlockSpec((bm, bn), lambda i, j: (i, j)),
        ],
        out_shape=[jax.ShapeDtypeStruct((m, n), out_dtypes[0]), jax.ShapeDtypeStruct((m, n), out_dtypes[1])],
        compiler_params=_params("parallel", "arbitrary"),
        name="heads_proj",
    )(xb, w, cos, sin)


def _gates_body(x_ref, w_ref, o_ref):
    o_ref[...] = _sigmoid(_dot(x_ref[...], w_ref[...]))


def _gates_proj(xb, w):
    m, d = xb.shape
    n = w.shape[-1]
    bm = _tile(m, 1024, SUBLANES * 2)
    return pl.pallas_call(
        _gates_body,
        grid=(m // bm,),
        in_specs=[pl.BlockSpec((bm, d), lambda i: (i, 0)), pl.BlockSpec((d, n), lambda i: (0, 0))],
        out_specs=pl.BlockSpec((bm, n), lambda i: (i, 0)),
        out_shape=jax.ShapeDtypeStruct((m, n), F32),
        compiler_params=_params("parallel"),
        name="gates_proj",
    )(xb, w)


def _gather_body(pt_ref, x_ref, oc_ref, os_ref):
    n_parts, n_kv = x_ref.shape[1], x_ref.shape[2]
    for part in range(n_parts):
        for g in range(n_kv):
            col = (part % (n_parts // 2)) * n_kv + g
            rows = x_ref[:, part, g, :]
            if part < n_parts // 2:
                oc_ref[:, col * LANES:(col + 1) * LANES] = rows
            else:
                os_ref[:, col * LANES:(col + 1) * LANES] = rows.astype(os_ref.dtype)


def _gather_pages(cache_kv, page_table):
    n_pool, page, n_parts, n_kv, hd = cache_kv.shape
    nb, n_pages = page_table.shape
    half = n_parts * n_kv * hd // 2
    grid_spec = pltpu.PrefetchScalarGridSpec(
        num_scalar_prefetch=1,
        grid=(nb, n_pages),
        in_specs=[pl.BlockSpec((None, page, n_parts, n_kv, hd), lambda b, p, pt: (pt[b, p], 0, 0, 0, 0))],
        out_specs=[
            pl.BlockSpec((None, page, half), lambda b, p, pt: (b, p, 0)),
            pl.BlockSpec((None, page, half), lambda b, p, pt: (b, p, 0)),
        ],
    )
    return pl.pallas_call(
        _gather_body,
        grid_spec=grid_spec,
        out_shape=[
            jax.ShapeDtypeStruct((nb, n_pages * page, half), F32),
            jax.ShapeDtypeStruct((nb, n_pages * page, half), BF16),
        ],
        compiler_params=_params("parallel", "arbitrary"),
        name="gather_pages",
    )(page_table, cache_kv)


def _compress_body(x_ref, pe_ref, w_ref, o_ref, *, nc):
    half = nc // 2
    acc = jnp.zeros((nc, LANES), F32)
    for l in range(CMP_BLOCK):
        xe = x_ref[pl.ds(l, half, stride=2 * CMP_BLOCK), :]
        xo = x_ref[pl.ds(l + CMP_BLOCK, half, stride=2 * CMP_BLOCK), :]
        xl = jnp.concatenate([xe, xo], axis=0) + pe_ref[l:l + 1, :]
        acc = acc + _dot(xl.astype(BF16), w_ref[l])
    o_ref[...] = acc.astype(o_ref.dtype)


def _compress(x3, pe, w, n_kv, nc):
    nb = x3.shape[0]
    tc = nc * CMP_BLOCK
    return pl.pallas_call(
        functools.partial(_compress_body, nc=nc),
        grid=(nb, 2, n_kv),
        in_specs=[
            pl.BlockSpec((None, tc, LANES), lambda b, c, g: (b, 0, c * n_kv + g)),
            pl.BlockSpec((None, CMP_BLOCK, LANES), lambda b, c, g: (c, 0, 0)),
            pl.BlockSpec((None, CMP_BLOCK, LANES, LANES), lambda b, c, g: (c, 0, 0, 0)),
        ],
        out_specs=pl.BlockSpec((None, None, None, nc, LANES), lambda b, c, g: (c, b, g, 0, 0)),
        out_shape=jax.ShapeDtypeStruct((2, nb, n_kv, nc, LANES), BF16),
        compiler_params=_params("parallel", "parallel", "arbitrary"),
        name="compress",
    )(x3, pe, w)


def _cmp_block_of_lane(nc, shape):
    lane = lax.broadcasted_iota(jnp.int32, shape, len(shape) - 1)
    half = nc // 2
    return jnp.where(lane < half, 2 * lane, 2 * (lane - half) + 1)


def _cmp_branch(q, kc, vc, trow, nc, scale):
    s = _dot_nt(q, kc) * scale
    n = _cmp_block_of_lane(nc, s.shape)
    ok = ((n + 1) * CMP_BLOCK - 1) <= trow
    sm = jnp.where(ok, s, NEG)
    e = jnp.exp(sm - jnp.max(sm, axis=-1, keepdims=True))
    p = jnp.where(ok, e / jnp.sum(e, axis=-1, keepdims=True), 0.0)
    return p, _dot(p.astype(BF16), vc)


def _online_update(carry, s, ok, v, scale):
    m, l, acc = carry
    c = scale * math.log2(math.e)
    sm = s if ok is None else jnp.where(ok, s, NEG)
    m_new = jnp.maximum(m, jnp.max(sm, axis=-1, keepdims=True))
    a = jnp.exp2((m - m_new) * c)
    p = jnp.exp2((sm - m_new) * c)
    return m_new, a * l + jnp.sum(p, axis=-1, keepdims=True), a * acc + _dot(p.astype(BF16), v)


def _online_init(rows):
    return (jnp.full((rows, 1), NEG, F32), jnp.zeros((rows, 1), F32), jnp.zeros((rows, LANES), F32))


def _block_bias(unsel_rows, k0, tk):
    nbk = unsel_rows.shape[-1]
    blk = lax.broadcasted_iota(jnp.int32, (nbk, tk), 0)
    kpos = k0 + lax.broadcasted_iota(jnp.int32, (nbk, tk), 1)
    expand = jnp.where(blk == kpos // SEL_BLOCK, -NEG, 0.0).astype(BF16)
    return _dot(unsel_rows, expand)


def _attn_prompt_body(q_ref, qr_ref, gt_ref, kc_ref, vc_ref, ks_ref, vs_ref, kw_ref, vw_ref, o_ref,
                      *, n_rep, tq, seq, nc, nsb, scale):
    t0 = pl.program_id(2) * tq
    rows = n_rep * tq
    q = jnp.concatenate([q_ref[:, r * LANES:(r + 1) * LANES] for r in range(n_rep)], axis=0)
    qr = jnp.concatenate([qr_ref[:, r * LANES:(r + 1) * LANES] for r in range(n_rep)], axis=0)
    trow = t0 + lax.rem(lax.broadcasted_iota(jnp.int32, (rows, 1), 0), tq)

    p_cmp, o_cmp = _cmp_branch(q, kc_ref[...], vc_ref[...], trow, nc, scale)
    imp = p_cmp[0:tq]
    for r in range(1, n_rep):
        imp = imp + p_cmp[r * tq:(r + 1) * tq]
    half = nc // 2
    imp = imp[:, :half] + imp[:, half:nc]
    imp = jnp.concatenate([imp, jnp.zeros((tq, LANES - half), F32)], axis=1)
    imp_t = imp.T[:nsb]

    j = lax.broadcasted_iota(jnp.int32, (nsb, tq), 0)
    tcol = t0 + lax.broadcasted_iota(jnp.int32, (nsb, tq), 1)
    cur = tcol // SEL_BLOCK
    forced = (j == 0) | (j == cur) | (j == cur - 1)
    avail = j * SEL_BLOCK <= tcol
    work = jnp.where(avail, imp_t + FORCE * forced.astype(F32), -1.0)
    sel = jnp.zeros((nsb, tq), F32)
    for _ in range(min(N_SEL, nsb)):
        mx = jnp.max(work, axis=0, keepdims=True)
        idx = jnp.min(jnp.where(work == mx, j, nsb), axis=0, keepdims=True)
        pick = j == idx
        sel = jnp.where(pick, 1.0, sel)
        work = jnp.where(pick, -jnp.inf, work)
    sel = jnp.where(avail, sel, 0.0)
    unsel = (jnp.concatenate([sel, jnp.zeros((LANES - nsb, tq), F32)], axis=0) - 1.0).T
    unsel = unsel.astype(BF16)

    tk = min(512, seq)
    n_tiles = (t0 + tq + tk - 1) // tk

    def sel_step(kt, carry):
        k0 = pl.multiple_of(kt * tk, tk)
        s = _dot_nt(qr, ks_ref[pl.ds(k0, tk), :])
        tpos = t0 + lax.broadcasted_iota(jnp.int32, (tq, tk), 0)
        kpos = k0 + lax.broadcasted_iota(jnp.int32, (tq, tk), 1)
        bias = _block_bias(unsel, k0, tk) + jnp.where(kpos <= tpos, 0.0, NEG)
        s = (s.reshape(n_rep, tq, tk) + bias[None]).reshape(rows, tk)
        return _online_update(carry, s, None, vs_ref[pl.ds(k0, tk), :], scale)

    _, l_sel, acc_sel = lax.fori_loop(0, n_tiles, sel_step, _online_init(rows))
    o_sel = acc_sel / l_sel

    wl = min(WINDOW + tq, seq)
    w0 = pl.multiple_of(jnp.maximum(t0 + tq - wl, 0), tq)
    s = _dot_nt(qr, kw_ref[pl.ds(w0, wl), :])
    dist = (t0 + lax.broadcasted_iota(jnp.int32, (tq, wl), 0)) - (w0 + lax.broadcasted_iota(jnp.int32, (tq, wl), 1))
    bias = jnp.where((dist >= 0) & (dist < WINDOW), 0.0, NEG)
    s = (s.reshape(n_rep, tq, wl) + bias[None]).reshape(rows, wl)
    _, l_win, acc_win = _online_update(_online_init(rows), s, None, vw_ref[pl.ds(w0, wl), :], scale)
    o_win = acc_win / l_win

    for r in range(n_rep):
        sl = slice(r * tq, (r + 1) * tq)
        o = (gt_ref[:, r:r + 1] * o_cmp[sl]
             + gt_ref[:, n_rep + r:n_rep + r + 1] * o_sel[sl]
             + gt_ref[:, 2 * n_rep + r:2 * n_rep + r + 1] * o_win[sl])
        o_ref[:, r * LANES:(r + 1) * LANES] = o.astype(o_ref.dtype)


def _attn_prompt(q, qr, gates_g, cmp_kv, kvb, nb, seq, n_kv):
    m, d = q.shape
    n_rep = d // (n_kv * LANES)
    nc = seq // CMP_BLOCK
    nsb = -(-seq // SEL_BLOCK)
    tq = _tile(seq, 128, LANES)
    nq = seq // tq
    hw = n_rep * LANES
    kv_spec = lambda part: pl.BlockSpec((seq, LANES), lambda b, g, i: (b, part * n_kv + g))
    cmp_spec = lambda c: pl.BlockSpec((None, None, None, nc, LANES), lambda b, g, i: (c, b, g, 0, 0))
    return pl.pallas_call(
        functools.partial(_attn_prompt_body, n_rep=n_rep, tq=tq, seq=seq, nc=nc, nsb=nsb,
                          scale=float(LANES) ** -0.5),
        grid=(nb, n_kv, nq),
        in_specs=[
            pl.BlockSpec((tq, hw), lambda b, g, i: (b * nq + i, g)),
            pl.BlockSpec((tq, hw), lambda b, g, i: (b * nq + i, g)),
            pl.BlockSpec((None, tq, 3 * n_rep), lambda b, g, i: (g, b * nq + i, 0)),
            cmp_spec(0), cmp_spec(1),
            kv_spec(2), kv_spec(3), kv_spec(4), kv_spec(5),
        ],
        out_specs=pl.BlockSpec((tq, hw), lambda b, g, i: (b * nq + i, g)),
        out_shape=jax.ShapeDtypeStruct((m, d), BF16),
        compiler_params=_params("parallel", "parallel", "arbitrary"),
        name="attn_prompt",
    )(q, qr, gates_g, cmp_kv, cmp_kv, kvb, kvb, kvb, kvb)


def _pad_rows(a, rows):
    return jnp.concatenate([a, jnp.zeros((rows - a.shape[0], a.shape[1]), a.dtype)], axis=0)


def _attn_sample_body(q_ref, qr_ref, gt_ref, kc_ref, vc_ref, ksp_ref, vsp_ref, ksn_ref, vsn_ref,
                      kwp_ref, vwp_ref, kwn_ref, vwn_ref, o_ref,
                      *, n_rep, tn, past, nc, nsb, nbk, scale):
    rows = n_rep * tn
    q = jnp.concatenate([q_ref[:, r * LANES:(r + 1) * LANES] for r in range(n_rep)], axis=0).astype(BF16)
    qr = jnp.concatenate([qr_ref[:, r * LANES:(r + 1) * LANES] for r in range(n_rep)], axis=0).astype(BF16)
    trow = past + lax.rem(lax.broadcasted_iota(jnp.int32, (rows, 1), 0), tn)

    p_cmp, o_cmp = _cmp_branch(q, kc_ref[...], vc_ref[...], trow, nc, scale)
    imp = p_cmp[0:tn]
    for r in range(1, n_rep):
        imp = imp + p_cmp[r * tn:(r + 1) * tn]
    half = nc // 2
    imp = imp[:, :half] + imp[:, half:nc]
    imp = jnp.concatenate([imp, jnp.zeros((tn, nbk - half), F32)], axis=1)

    j = lax.broadcasted_iota(jnp.int32, (tn, nbk), 1)
    tcol = past + lax.broadcasted_iota(jnp.int32, (tn, nbk), 0)
    cur = tcol // SEL_BLOCK
    forced = (j == 0) | (j == cur) | (j == cur - 1)
    avail = (j * SEL_BLOCK <= tcol) & (j < nsb)
    work = jnp.where(avail, imp + FORCE * forced.astype(F32), -1.0)
    work = jnp.where(j < nsb, work, -jnp.inf)
    sel = jnp.zeros((tn, nbk), F32)
    for _ in range(min(N_SEL, nsb)):
        mx = jnp.max(work, axis=1, keepdims=True)
        idx = jnp.min(jnp.where(work == mx, j, nbk), axis=1, keepdims=True)
        pick = j == idx
        sel = jnp.where(pick, 1.0, sel)
        work = jnp.where(pick, -jnp.inf, work)
    sel = jnp.where(avail, sel, 0.0)
    unsel_rows = jnp.concatenate([(sel - 1.0).astype(BF16)] * n_rep, axis=0)

    tk = min(1024, past)

    def sel_step(kt, carry):
        k0 = pl.multiple_of(kt * tk, tk)
        s = _dot_nt(qr, ksp_ref[pl.ds(k0, tk), :]) + _block_bias(unsel_rows, k0, tk)
        kpos = k0 + lax.broadcasted_iota(jnp.int32, (rows, tk), 1)
        return _online_update(carry, s, kpos <= trow, vsp_ref[pl.ds(k0, tk), :], scale)

    carry = lax.fori_loop(0, past // tk, sel_step, _online_init(rows))
    kn = _pad_rows(ksn_ref[...], LANES).astype(BF16)
    vn = _pad_rows(vsn_ref[...], LANES).astype(BF16)
    kpos = past + lax.broadcasted_iota(jnp.int32, (rows, LANES), 1)
    s = _dot_nt(qr, kn) + _block_bias(unsel_rows, past, LANES)
    _, l_sel, acc_sel = _online_update(carry, s, kpos <= trow, vn, scale)
    o_sel = acc_sel / l_sel

    wb = kwp_ref.shape[0]
    dist = trow - (past - wb + lax.broadcasted_iota(jnp.int32, (rows, wb), 1))
    ok = (dist >= 0) & (dist < WINDOW)
    carry = _online_update(_online_init(rows), _dot_nt(qr, kwp_ref[...].astype(BF16)), ok,
                           vwp_ref[...].astype(BF16), scale)
    kn = _pad_rows(kwn_ref[...], LANES).astype(BF16)
    vn = _pad_rows(vwn_ref[...], LANES).astype(BF16)
    dist = trow - kpos
    ok = (dist >= 0) & (dist < WINDOW)
    _, l_win, acc_win = _online_update(carry, _dot_nt(qr, kn), ok, vn, scale)
    o_win = acc_win / l_win

    for r in range(n_rep):
        sl = slice(r * tn, (r + 1) * tn)
        o = (gt_ref[:, r:r + 1] * o_cmp[sl]
             + gt_ref[:, n_rep + r:n_rep + r + 1] * o_sel[sl]
             + gt_ref[:, 2 * n_rep + r:2 * n_rep + r + 1] * o_win[sl])
        o_ref[:, r * LANES:(r + 1) * LANES] = o


def _attn_sample(q, qr, gates_g, cmp_kv, sel_past, kv_new, win_past, nb, tn, n_kv):
    m, d = q.shape
    n_rep = d // (n_kv * LANES)
    past = sel_past.shape[1]
    nc = cmp_kv.shape[3]
    nsb = -(-(past + tn) // SEL_BLOCK)
    nbk = -(-nsb // LANES) * LANES
    wb = win_past.shape[1]
    hw = n_rep * LANES
    new_spec = lambda part: pl.BlockSpec((tn, LANES), lambda b, g: (b, part * n_kv + g))
    cmp_spec = lambda c: pl.BlockSpec((None, None, None, nc, LANES), lambda b, g: (c, b, g, 0, 0))
    return pl.pallas_call(
        functools.partial(_attn_sample_body, n_rep=n_rep, tn=tn, past=past, nc=nc, nsb=nsb, nbk=nbk,
                          scale=float(LANES) ** -0.5),
        grid=(nb, n_kv),
        in_specs=[
            pl.BlockSpec((tn, hw), lambda b, g: (b, g)),
            pl.BlockSpec((tn, hw), lambda b, g: (b, g)),
            pl.BlockSpec((None, tn, 3 * n_rep), lambda b, g: (g, b, 0)),
            cmp_spec(0), cmp_spec(1),
            pl.BlockSpec((None, past, LANES), lambda b, g: (b, 0, g)),
            pl.BlockSpec((None, past, LANES), lambda b, g: (b, 0, n_kv + g)),
            new_spec(2), new_spec(3),
            pl.BlockSpec((None, wb, LANES), lambda b, g: (b, 0, g)),
            pl.BlockSpec((None, wb, LANES), lambda b, g: (b, 0, n_kv + g)),
            new_spec(4), new_spec(5),
        ],
        out_specs=pl.BlockSpec((tn, hw), lambda b, g: (b, g)),
        out_shape=jax.ShapeDtypeStruct((m, d), F32),
        compiler_params=_params("parallel", "arbitrary"),
        name="attn_sample",
    )(q, qr, gates_g, cmp_kv, cmp_kv, sel_past, sel_past, kv_new, kv_new, win_past, win_past, kv_new, kv_new)


def _rope_tables(pos):
    half = LANES // 2
    inv = ROPE_THETA ** (-jnp.arange(half, dtype=F32) / half)
    ang = pos.astype(F32)[:, None] * inv[None, :]
    cos, sin = jnp.cos(ang), jnp.sin(ang)
    return jnp.concatenate([cos, cos], axis=-1), jnp.concatenate([-sin, sin], axis=-1)


def _prep_weights(a_w_in, w_kv, cmp_pe, cmp_w, b_w_qg, ffn_w_down, ple_w_proj, ple_w_gate, n_kv):
    d = a_w_in.shape[1]
    n_heads = d // LANES
    n_rep = n_heads // n_kv
    f = ffn_w_down.shape[1]
    fp = -(-f // 1024) * 1024
    w = {}
    w["a_w_in"] = a_w_in.astype(BF16)
    w["w_kv"] = w_kv.astype(BF16)
    w["cmp_pe"] = jnp.transpose(cmp_pe, (1, 0, 2))
    w["cmp_w"] = jnp.transpose(cmp_w, (1, 0, 2, 3)).astype(BF16)
    w["w_q"] = b_w_qg[:, :, :d].astype(BF16)
    wg = b_w_qg[:, :, d:].reshape(-1, d, n_kv, n_rep, 3)
    wg = jnp.transpose(wg, (0, 1, 2, 4, 3)).reshape(-1, d, 3 * n_heads)
    w["w_gates"] = jnp.pad(wg, ((0, 0), (0, 0), (0, LANES - 3 * n_heads))).astype(BF16)
    w["ffn_w_down"] = jnp.pad(ffn_w_down, ((0, 0), (0, fp - f), (0, 0))).astype(BF16)
    w["ffn_f"] = fp
    w["ple_w_proj"] = ple_w_proj.astype(BF16)
    w["ple_w_gate"] = ple_w_gate.astype(BF16)
    return w


def _new_stream(x3, p4, conv_prev, past_ctx):
    nb, seq, d = x3.shape
    m = nb * seq
    past = 0 if past_ctx is None else past_ctx["past"]
    cos1, sin1 = _rope_tables(past + jnp.arange(seq, dtype=jnp.int32))
    x = x3.reshape(m, d)
    return dict(nb=nb, seq=seq, m=m, x=x, xb=x.astype(BF16), pb=p4.reshape(p4.shape[0], m, -1).astype(BF16),
                conv_prev=conv_prev, past_ctx=past_ctx, cos=jnp.tile(cos1, (nb, 1)), sin=jnp.tile(sin1, (nb, 1)),
                conv_states=[], ctx=None)


def _conv_mixer(s, i, w, raw, ln_g, ln_b, alpha):
    d = s["x"].shape[1]
    prev = jnp.zeros((s["nb"], CONV_W - 1, d), F32) if s["conv_prev"] is None else s["conv_prev"][i]
    v, st = _conv_in(s["xb"], w["a_w_in"][i], raw["a_conv_w"][i], prev, s["seq"])
    s["conv_states"].append(st[:, SUBLANES - (CONV_W - 1):])
    s["x"], s["xb"] = _proj_ln_ws(v, raw["a_w_out"][:, None], i, s["x"], ln_g, ln_b, alpha)


def _nsa_mixer(s, li, w, raw, ln_g, ln_b, alpha, n_kv):
    m, d = s["x"].shape
    n_heads = d // LANES
    n_rep = n_heads // n_kv
    prompt = s["past_ctx"] is None
    qdt = BF16 if prompt else F32
    q, qr = _heads_proj(s["xb"], w["w_q"][li], s["cos"], s["sin"], two_out=True, rope_blocks=(),
                        out_dtypes=(qdt, qdt))
    gates = _gates_proj(s["xb"], w["w_gates"][li])[:, :3 * n_heads]
    gates_g = jnp.transpose(gates.reshape(m, n_kv, 3 * n_rep), (1, 0, 2))
    ctx = s["ctx"]
    if prompt:
        o = _attn_prompt(q, qr, gates_g, ctx["cmp_kv"], ctx["kvb"], s["nb"], s["seq"], n_kv)
    else:
        o = _attn_sample(q, qr, gates_g, ctx["cmp_kv"], s["past_ctx"]["sel_past"], ctx["kv_all"],
                         s["past_ctx"]["win_past"], s["nb"], s["seq"], n_kv).astype(BF16)
    s["x"], s["xb"] = _proj_ln_ws(o, raw["b_w_o"][:, None], li, s["x"], ln_g, ln_b, alpha)


def _shared_context(s, w, n_kv):
    kv_all, kvb = _heads_proj(s["xb"], w["w_kv"], s["cos"], s["sin"], two_out=False, rope_blocks=(2, 4),
                              out_dtypes=(F32, BF16))
    if s["past_ctx"] is None:
        cmp_kv = _compress(kv_all.reshape(s["nb"], s["seq"], -1), w["cmp_pe"], w["cmp_w"], n_kv,
                           s["seq"] // CMP_BLOCK)
    else:
        past = s["past_ctx"]["past"]
        assert past % CMP_BLOCK == 0 and s["seq"] < CMP_BLOCK
        cmp_kv = _compress(s["past_ctx"]["cmp_past"], w["cmp_pe"], w["cmp_w"], n_kv, past // CMP_BLOCK)
    s["ctx"] = dict(kv_all=kv_all, kvb=kvb, cmp_kv=cmp_kv)


def _trunk(streams, w, raw, n_kv):
    depth = raw["ln_g"].shape[0]
    n_a = depth // 2
    alpha = float((2 * depth) ** 0.25)
    for i in range(depth):
        ln_g, ln_b = raw["ln_g"][i], raw["ln_b"][i]
        for s in streams:
            if i < n_a:
                _conv_mixer(s, i, w, raw, ln_g[0], ln_b[0], alpha)
            else:
                _nsa_mixer(s, i - n_a, w, raw, ln_g[0], ln_b[0], alpha, n_kv)
        if i % 2 == 0:
            for s in streams:
                h = _swiglu_ws(s["xb"], raw["ffn_w_gu"][:, None], i // 2, _tile(s["m"], 1024, SUBLANES * 2),
                               f_pad=w["ffn_f"])
                s["x"], s["xb"] = _proj_ln(h, w["ffn_w_down"][i // 2], s["x"], ln_g[1], ln_b[1], alpha)
        else:
            _moe_layer(streams, i // 2, raw, w, ln_g[1], ln_b[1], alpha)
        for s in streams:
            s["x"], s["xb"] = _ple(s["x"], s["xb"], s["pb"][i], w["ple_w_proj"][i], w["ple_w_gate"][i])
            if i == n_a - 1:
                _shared_context(s, w, n_kv)
    outs = []
    for s in streams:
        nb, seq, hd_blk = s["nb"], s["seq"], n_kv * LANES
        kv_all = s["ctx"]["kv_all"]
        rows = kv_all[:, :4 * hd_blk].reshape(nb, seq, 4, n_kv, LANES)
        win = kv_all[:, 4 * hd_blk:].reshape(nb, seq, 2, n_kv, LANES)
        outs.append((s["x"].reshape(nb, seq, -1), rows, win, jnp.stack(s["conv_states"])))
    return outs


def kernel(x_prompt, x_sample, cache_kv, cache_win, state_conv, page_table, p_prompt, p_sample,
           a_w_in, a_conv_w, a_w_out, w_kv, cmp_pe, cmp_w, b_w_qg, b_w_o, ffn_w_gu, ffn_w_down,
           moe_w_router, moe_b_router, moe_w_gu, moe_w_down, ple_w_proj, ple_w_gate, ln_g, ln_b):
    n_kv = cache_kv.shape[3]
    raw = dict(a_conv_w=a_conv_w, a_w_out=a_w_out, b_w_o=b_w_o, ffn_w_gu=ffn_w_gu, moe_w_router=moe_w_router,
               moe_b_router=moe_b_router, moe_w_gu=moe_w_gu, moe_w_down=moe_w_down, ln_g=ln_g, ln_b=ln_b)
    w = _prep_weights(a_w_in, w_kv, cmp_pe, cmp_w, b_w_qg, ffn_w_down, ple_w_proj, ple_w_gate, n_kv)

    n_pool, page = cache_kv.shape[:2]
    cmp_past, sel_past = _gather_pages(cache_kv, page_table)
    nb_s, wb = cache_win.shape[:2]
    past_ctx = dict(past=page_table.shape[1] * page, cmp_past=cmp_past, sel_past=sel_past,
                    win_past=cache_win.reshape(nb_s, wb, -1))

    streams = [_new_stream(x_prompt, p_prompt, None, None),
               _new_stream(x_sample, p_sample, state_conv, past_ctx)]
    (y_p, rows_p, win_p, conv_p), (y_s, rows_s, win_s, conv_s) = _trunk(streams, w, raw, n_kv)
    win_state_p = win_p[:, -min(WINDOW, x_prompt.shape[1]):]
    win_state_s = jnp.concatenate([cache_win.astype(win_s.dtype), win_s], axis=1)[:, -wb:]

    return (y_p, y_s, rows_p, rows_s, win_state_p, win_state_s, conv_p, conv_s)
```

```python
import functools
import math

import jax
import jax.numpy as jnp
from jax import lax
from jax.experimental import pallas as pl
from jax.experimental.pallas import tpu as pltpu

F32 = jnp.float32
BF16 = jnp.bfloat16

CONV_W = 3
CMP_BLOCK = 32
SEL_BLOCK = 64
N_SEL = 16
WINDOW = 512
ROPE_THETA = 10000.0
LN_EPS = 1e-5
NEG = -1e30
FORCE = 1e4

LANES = 128
SUBLANES = 8
VMEM_LIMIT_BYTES = 56 * 1024 * 1024


def _params(*sem):
    return pltpu.CompilerParams(dimension_semantics=sem, vmem_limit_bytes=VMEM_LIMIT_BYTES)


def _tile(n, pref, align):
    if n <= pref:
        return n
    t = (pref // align) * align
    while t >= align:
        if n % t == 0:
            return t
        t -= align
    raise ValueError(f"no tile for {n} (pref {pref}, align {align})")


def _dot(a, b):
    return jnp.dot(a, b, preferred_element_type=F32)


def _dot_nt(a, b):
    return lax.dot_general(a, b, (((1,), (1,)), ((), ())), preferred_element_type=F32)


def _sigmoid(x):
    return 1.0 / (1.0 + jnp.exp(-x))


def _rope_head(a, cos, sin_signed):
    return a * cos + pltpu.roll(a, a.shape[-1] // 2, axis=1) * sin_signed


def _layer_norm(z, g, b):
    mu = jnp.mean(z, axis=-1, keepdims=True)
    zc = z - mu
    var = jnp.mean(zc * zc, axis=-1, keepdims=True)
    return zc * lax.rsqrt(var + LN_EPS) * g + b


WS_STAGE_BYTES = 4 * 1024 * 1024


def _ws_body(te_ref, nu_ref, x_ref, w_hbm, *rest, layer, seg_offsets, bn, ck, nj_valid, epilogue, n_extra, n_out):
    extra = rest[:n_extra]
    outs = rest[n_extra:n_extra + n_out]
    cache, stage, acc_ref, sem = rest[n_extra + n_out:]
    j = pl.program_id(0)
    i = pl.program_id(1)
    e = te_ref[i]
    nseg = len(seg_offsets)
    n_chunks = x_ref.shape[1] // ck
    col_ok = j < nj_valid
    valid = jnp.logical_and(i < nu_ref[0], col_ok)
    changed = jnp.logical_and(jnp.logical_or(i == 0, e != te_ref[jnp.maximum(i - 1, 0)]), col_ok)

    def copies(c, slot):
        return [pltpu.make_async_copy(
            w_hbm.at[layer, e, pl.ds(c * ck, ck), pl.ds(pl.multiple_of(off + j * bn, LANES), bn)],
            stage.at[slot, s], sem.at[slot]) for s, off in enumerate(seg_offsets)]

    @pl.when(changed)
    def _():
        for cp in copies(0, 0):
            cp.start()
        for c in range(n_chunks):
            slot = c % 2
            if c + 1 < n_chunks:
                for cp in copies(c + 1, 1 - slot):
                    cp.start()
            for cp in copies(c, slot):
                cp.wait()
            for s in range(nseg):
                cache[c * ck:(c + 1) * ck, s * bn:(s + 1) * bn] = stage[slot, s].astype(BF16)
            part = _dot(x_ref[:, c * ck:(c + 1) * ck], cache[c * ck:(c + 1) * ck, :])
            if c == 0:
                acc_ref[...] = part
            else:
                acc_ref[...] += part

    @pl.when(jnp.logical_and(valid, jnp.logical_not(changed)))
    def _():
        epilogue(lambda c0, c1: _dot(x_ref[...], cache[:, c0:c1]), acc_ref, extra, outs)

    @pl.when(jnp.logical_and(valid, changed))
    def _():
        epilogue(lambda c0, c1: acc_ref[:, c0:c1], acc_ref, extra, outs)

    @pl.when(jnp.logical_not(valid))
    def _():
        for o in outs:
            o[...] = jnp.zeros_like(o)


def _ws_matmul(x, w4, layer, seg_offsets, bn, nj, nj_valid, bm, epilogue, extra, extra_specs, out_shape, out_specs,
               name, meta=None):
    rows, k = x.shape
    n_tiles = rows // bm
    nseg = len(seg_offsets)
    ck = _tile(k, max(LANES, WS_STAGE_BYTES // (2 * nseg * bn * 4)), LANES)
    if meta is None:
        te = jnp.zeros((n_tiles,), jnp.int32)
        nu = jnp.full((1,), n_tiles, jnp.int32)
    else:
        te, nu = meta["tile_expert"], meta["n_used"]
    grid_spec = pltpu.PrefetchScalarGridSpec(
        num_scalar_prefetch=2,
        grid=(nj, n_tiles),
        in_specs=[pl.BlockSpec((bm, k), lambda j, i, te, nu: (i, 0)), pl.BlockSpec(memory_space=pl.ANY)]
        + list(extra_specs),
        out_specs=out_specs,
        scratch_shapes=[pltpu.VMEM((k, nseg * bn), BF16), pltpu.VMEM((2, nseg, ck, bn), F32),
                        pltpu.VMEM((bm, nseg * bn), F32), pltpu.SemaphoreType.DMA((2,))],
    )
    return pl.pallas_call(
        functools.partial(_ws_body, layer=layer, seg_offsets=tuple(seg_offsets), bn=bn, ck=ck, nj_valid=nj_valid,
                          epilogue=epilogue, n_extra=len(extra), n_out=len(out_specs)),
        grid_spec=grid_spec,
        out_shape=out_shape,
        compiler_params=_params("arbitrary", "arbitrary"),
        name=name,
    )(te, nu, x, w4, *extra)


EPI_COLS = 256


def _epi_swiglu(mm, acc_ref, extra, outs):
    bn = outs[0].shape[-1]
    cw = _tile(bn, EPI_COLS, LANES)
    for c in range(bn // cw):
        g = mm(c * cw, (c + 1) * cw)
        u = mm(bn + c * cw, bn + (c + 1) * cw)
        outs[0][:, c * cw:(c + 1) * cw] = (g * _sigmoid(g) * u).astype(outs[0].dtype)


def _epi_copy(mm, acc_ref, extra, outs):
    bn = outs[0].shape[-1]
    cw = _tile(bn, EPI_COLS, LANES)
    for c in range(bn // cw):
        outs[0][:, c * cw:(c + 1) * cw] = mm(c * cw, (c + 1) * cw)


def _epi_ln(mm, acc_ref, extra, outs, *, alpha, br):
    res_ref, g_ref, b_ref = extra
    o_ref, ob_ref = outs
    acc_ref[...] = mm(0, acc_ref.shape[1])

    def rows(r, carry):
        sl = pl.ds(pl.multiple_of(r * br, br), br)
        y = _layer_norm(alpha * res_ref[sl, :] + acc_ref[sl, :], g_ref[...], b_ref[...])
        o_ref[sl, :] = y
        ob_ref[sl, :] = y.astype(ob_ref.dtype)
        return carry

    lax.fori_loop(0, o_ref.shape[0] // br, rows, 0)


def _swiglu_ws(x, w_gu4, layer, bm, meta=None, f_pad=None):
    rows = x.shape[0]
    f = w_gu4.shape[-1] // 2
    f_out = f if f_pad is None else f_pad
    bn = _tile(f, 1024, LANES) if f % 1024 == 0 else _tile(f, 256, LANES)
    return _ws_matmul(
        x, w_gu4, layer, (0, f), bn, f_out // bn, f // bn, bm, _epi_swiglu, (), (),
        [jax.ShapeDtypeStruct((rows, f_out), BF16)],
        [pl.BlockSpec((bm, bn), lambda j, i, te, nu: (i, j))], "swiglu_ws", meta)[0]


def _proj_ln_ws(lhs, w4, layer, res, g, b, alpha):
    m, d = res.shape
    bm = _tile(m, 128, SUBLANES * 2)
    br = _tile(bm, 32, SUBLANES * 2)
    row_spec = pl.BlockSpec((bm, d), lambda j, i, te, nu: (i, 0))
    vec_spec = pl.BlockSpec((1, d), lambda j, i, te, nu: (0, 0))
    return _ws_matmul(
        lhs, w4, layer, (0,), d, 1, 1, bm, functools.partial(_epi_ln, alpha=alpha, br=br),
        (res, g.reshape(1, d), b.reshape(1, d)), (row_spec, vec_spec, vec_spec),
        [jax.ShapeDtypeStruct((m, d), F32), jax.ShapeDtypeStruct((m, d), BF16)],
        [row_spec, row_spec], "proj_ln_ws")


def _conv_in_body(x_ref, wb_ref, wc_ref, wh_ref, cw_ref, prev_ref, v_ref, st_ref, conv_ref, *, seq, nseq):
    x = x_ref[...]
    b = _dot(x, wb_ref[...])
    u = _dot(x, wc_ref[...]) * _dot(x, wh_ref[...])
    w0 = cw_ref[0:1, :]
    w1 = cw_ref[1:2, :]
    w2 = cw_ref[2:3, :]
    for s in range(nseq):
        us = u[s * seq:(s + 1) * seq]
        tt = lax.broadcasted_iota(jnp.int32, us.shape, 0)
        p0 = prev_ref[s, 0:1, :]
        p1 = prev_ref[s, 1:2, :]
        u1 = jnp.where(tt >= 1, pltpu.roll(us, 1, axis=0), p1)
        u2 = jnp.where(tt >= 2, pltpu.roll(us, 2, axis=0), jnp.where(tt == 0, p0, p1))
        conv_ref[s * seq:(s + 1) * seq, :] = u2 * w0 + u1 * w1 + us * w2
        st_ref[s] = us[seq - SUBLANES:seq]
    v_ref[...] = (b * conv_ref[...]).astype(v_ref.dtype)


def _conv_in(xb, w_in, conv_w, prev, seq):
    m, d = xb.shape
    nb_seq = m // seq
    nseq = 1 if seq >= 512 else nb_seq
    bm = nseq * seq
    bn = _tile(d, 256, LANES)
    nb = d // bn
    x_mode = pl.Buffered(1) if bm * d * 2 > (8 << 20) else None
    return pl.pallas_call(
        functools.partial(_conv_in_body, seq=seq, nseq=nseq),
        grid=(m // bm, nb),
        in_specs=[
            pl.BlockSpec((bm, d), lambda i, j: (i, 0), pipeline_mode=x_mode),
            pl.BlockSpec((d, bn), lambda i, j: (0, j)),
            pl.BlockSpec((d, bn), lambda i, j: (0, j + nb)),
            pl.BlockSpec((d, bn), lambda i, j: (0, j + 2 * nb)),
            pl.BlockSpec((CONV_W, bn), lambda i, j: (0, j)),
            pl.BlockSpec((nseq, CONV_W - 1, bn), lambda i, j: (i, 0, j)),
        ],
        out_specs=[
            pl.BlockSpec((bm, bn), lambda i, j: (i, j)),
            pl.BlockSpec((nseq, SUBLANES, bn), lambda i, j: (i, 0, j)),
        ],
        out_shape=[
            jax.ShapeDtypeStruct((m, d), BF16),
            jax.ShapeDtypeStruct((nb_seq, SUBLANES, d), F32),
        ],
        scratch_shapes=[pltpu.VMEM((bm, bn), F32)],
        compiler_params=_params("parallel", "arbitrary"),
        name="conv_in",
    )(xb, w_in, w_in, w_in, conv_w, prev)


def _proj_ln_body(lhs_ref, w_ref, res_ref, g_ref, b_ref, o_ref, ob_ref, *, nk, alpha, bn, br):
    k = pl.program_id(1)
    bm, d = o_ref.shape

    @pl.when(k == 0)
    def _():
        o_ref[...] = jnp.zeros_like(o_ref)

    lhs = lhs_ref[...]
    for c in range(d // bn):
        o_ref[:, c * bn:(c + 1) * bn] += _dot(lhs, w_ref[:, c * bn:(c + 1) * bn])

    @pl.when(k == nk - 1)
    def _():
        def rows(r, carry):
            sl = pl.ds(pl.multiple_of(r * br, br), br)
            y = _layer_norm(alpha * res_ref[sl, :] + o_ref[sl, :], g_ref[...], b_ref[...])
            o_ref[sl, :] = y
            ob_ref[sl, :] = y.astype(ob_ref.dtype)
            return carry

        lax.fori_loop(0, bm // br, rows, 0)


def _proj_ln(lhs, w, res, g, b, alpha):
    m, kdim = lhs.shape
    d = w.shape[-1]
    bm = _tile(m, 512, SUBLANES * 2)
    bk = _tile(kdim, 1024, LANES)
    nk = kdim // bk
    bn = _tile(d, 512, LANES)
    br = _tile(bm, 32, SUBLANES * 2)
    return pl.pallas_call(
        functools.partial(_proj_ln_body, nk=nk, alpha=alpha, bn=bn, br=br),
        grid=(m // bm, nk),
        in_specs=[
            pl.BlockSpec((bm, bk), lambda i, k: (i, k)),
            pl.BlockSpec((bk, d), lambda i, k: (k, 0)),
            pl.BlockSpec((bm, d), lambda i, k: (i, 0), pipeline_mode=pl.Buffered(1)),
            pl.BlockSpec((1, d), lambda i, k: (0, 0)),
            pl.BlockSpec((1, d), lambda i, k: (0, 0)),
        ],
        out_specs=[
            pl.BlockSpec((bm, d), lambda i, k: (i, 0)),
            pl.BlockSpec((bm, d), lambda i, k: (i, 0)),
        ],
        out_shape=[jax.ShapeDtypeStruct((m, d), F32), jax.ShapeDtypeStruct((m, d), BF16)],
        compiler_params=_params("parallel", "arbitrary"),
        name="proj_ln",
    )(lhs, w, res, g.reshape(1, d), b.reshape(1, d))


def _router_body(x_ref, w_ref, b_ref, o_ref, *, n_experts):
    logits = jnp.dot(x_ref[...], w_ref[...], preferred_element_type=F32,
                     precision=lax.Precision.HIGHEST) + b_ref[...]
    lane = lax.broadcasted_iota(jnp.int32, logits.shape, 1)
    lg = jnp.where(lane < n_experts, logits, -jnp.inf)
    m1 = jnp.max(lg, axis=-1, keepdims=True)
    i1 = jnp.min(jnp.where(lg == m1, lane, LANES), axis=-1, keepdims=True)
    lg2 = jnp.where(lane == i1, -jnp.inf, lg)
    m2 = jnp.max(lg2, axis=-1, keepdims=True)
    i2 = jnp.min(jnp.where(lg2 == m2, lane, LANES), axis=-1, keepdims=True)
    e2 = jnp.exp(m2 - m1)
    w1 = 1.0 / (1.0 + e2)
    w2 = e2 / (1.0 + e2)
    o_ref[...] = jnp.where(lane == 0, w1, jnp.where(lane == 1, w2, jnp.where(
        lane == 2, i1.astype(F32), jnp.where(lane == 3, i2.astype(F32), 0.0))))


def _router(x, w_router, b_router):
    m, d = x.shape
    n_experts = w_router.shape[-1]
    w = jnp.pad(w_router, ((0, 0), (0, LANES - n_experts)))
    b = jnp.pad(b_router, (0, LANES - n_experts)).reshape(1, LANES)
    bm = _tile(m, 512, SUBLANES)
    return pl.pallas_call(
        functools.partial(_router_body, n_experts=n_experts),
        grid=(m // bm,),
        in_specs=[
            pl.BlockSpec((bm, d), lambda i: (i, 0)),
            pl.BlockSpec((d, LANES), lambda i: (0, 0)),
            pl.BlockSpec((1, LANES), lambda i: (0, 0)),
        ],
        out_specs=pl.BlockSpec((bm, LANES), lambda i: (i, 0)),
        out_shape=jax.ShapeDtypeStruct((m, LANES), F32),
        compiler_params=_params("parallel"),
        name="router",
    )(x, w, b)


MOE_ROW_TILE = 512


def _route_meta(eid, n_exp, bm):
    m = eid.shape[0]
    a = 2 * m
    n_tiles = -(-a // bm) + n_exp
    e_flat = eid.reshape(a)
    order = jnp.argsort(e_flat, stable=True).astype(jnp.int32)
    e_sorted = e_flat[order]
    counts = jnp.sum(e_flat[:, None] == jnp.arange(n_exp, dtype=jnp.int32)[None, :], axis=0).astype(jnp.int32)
    tiles_e = (counts + bm - 1) // bm
    tile_end = jnp.cumsum(tiles_e).astype(jnp.int32)
    tile_start = tile_end - tiles_e
    first = jnp.cumsum(counts).astype(jnp.int32) - counts
    pos_sorted = tile_start[e_sorted] * bm + (jnp.arange(a, dtype=jnp.int32) - first[e_sorted])
    row_token = jnp.zeros((n_tiles * bm,), jnp.int32).at[pos_sorted].set(order // 2)
    pos = jnp.zeros((a,), jnp.int32).at[order].set(pos_sorted)
    tile_expert = jnp.minimum(jnp.searchsorted(tile_end, jnp.arange(n_tiles, dtype=jnp.int32), side="right"),
                              n_exp - 1).astype(jnp.int32)
    return dict(row_token=row_token, pos=pos, tile_expert=tile_expert, n_used=tile_end[-1:], n_tiles=n_tiles)


def _row_copy(src_hbm, row, dst, dst_row, sem):
    return pltpu.make_async_copy(src_hbm.at[pl.ds(row, 1), :], dst.at[pl.ds(dst_row, 1), :], sem)


def _moe_gather_body(tok_ref, x_hbm, o_ref, buf, sem, *, bm):
    base = pl.program_id(0) * bm

    def issue(r, carry):
        _row_copy(x_hbm, tok_ref[base + r], buf, r, sem).start()
        return carry

    def drain(r, carry):
        _row_copy(x_hbm, 0, buf, r, sem).wait()
        return carry

    lax.fori_loop(0, bm, issue, 0, unroll=8)
    lax.fori_loop(0, bm, drain, 0, unroll=8)
    o_ref[...] = buf[...].astype(o_ref.dtype)


def _moe_gather(row_token, x_all):
    d = x_all.shape[1]
    rows = row_token.shape[0]
    bm = _tile(rows, 256, SUBLANES * 2)
    grid_spec = pltpu.PrefetchScalarGridSpec(
        num_scalar_prefetch=1,
        grid=(rows // bm,),
        in_specs=[pl.BlockSpec(memory_space=pl.ANY)],
        out_specs=pl.BlockSpec((bm, d), lambda i, tok: (i, 0)),
        scratch_shapes=[pltpu.VMEM((bm, d), F32), pltpu.SemaphoreType.DMA],
    )
    return pl.pallas_call(
        functools.partial(_moe_gather_body, bm=bm),
        grid_spec=grid_spec,
        out_shape=jax.ShapeDtypeStruct((rows, d), BF16),
        compiler_params=_params("arbitrary"),
        name="moe_gather",
    )(row_token, x_all)


def _moe_down(meta, h, w_down4, layer, bm):
    rows = h.shape[0]
    d = w_down4.shape[-1]
    bn = _tile(d, 1024, LANES)
    return _ws_matmul(
        h, w_down4, layer, (0,), bn, d // bn, d // bn, bm, _epi_copy, (), (),
        [jax.ShapeDtypeStruct((rows, d), F32)],
        [pl.BlockSpec((bm, bn), lambda j, i, te, nu: (i, j))], "moe_down", meta)[0]


def _moe_combine_body(pos_ref, y_hbm, w_ref, res_ref, g_ref, b_ref, o_ref, ob_ref, buf, sem, *, bm, br, alpha):
    base = pl.program_id(0) * bm

    def issue(r, carry):
        for k in range(2):
            _row_copy(y_hbm, pos_ref[2 * (base + r) + k], buf.at[k], r, sem).start()
        return carry

    def drain(r, carry):
        for k in range(2):
            _row_copy(y_hbm, 0, buf.at[k], r, sem).wait()
        return carry

    lax.fori_loop(0, bm, issue, 0, unroll=8)
    lax.fori_loop(0, bm, drain, 0, unroll=8)

    def rows(r, carry):
        sl = pl.ds(pl.multiple_of(r * br, br), br)
        f = w_ref[sl, 0:1] * buf[0, sl, :] + w_ref[sl, 1:2] * buf[1, sl, :]
        y = _layer_norm(alpha * res_ref[sl, :] + f, g_ref[...], b_ref[...])
        o_ref[sl, :] = y
        ob_ref[sl, :] = y.astype(ob_ref.dtype)
        return carry

    lax.fori_loop(0, bm // br, rows, 0)


def _moe_combine(pos, y, route, res, g, b, alpha):
    m, d = res.shape
    bm = _tile(m, 128, SUBLANES * 2)
    br = _tile(bm, 32, SUBLANES * 2)
    grid_spec = pltpu.PrefetchScalarGridSpec(
        num_scalar_prefetch=1,
        grid=(m // bm,),
        in_specs=[
            pl.BlockSpec(memory_space=pl.ANY),
            pl.BlockSpec((bm, LANES), lambda i, p: (i, 0)),
            pl.BlockSpec((bm, d), lambda i, p: (i, 0)),
            pl.BlockSpec((1, d), lambda i, p: (0, 0)),
            pl.BlockSpec((1, d), lambda i, p: (0, 0)),
        ],
        out_specs=[
            pl.BlockSpec((bm, d), lambda i, p: (i, 0)),
            pl.BlockSpec((bm, d), lambda i, p: (i, 0)),
        ],
        scratch_shapes=[pltpu.VMEM((2, bm, d), F32), pltpu.SemaphoreType.DMA],
    )
    return pl.pallas_call(
        functools.partial(_moe_combine_body, bm=bm, br=br, alpha=alpha),
        grid_spec=grid_spec,
        out_shape=[jax.ShapeDtypeStruct((m, d), F32), jax.ShapeDtypeStruct((m, d), BF16)],
        compiler_params=_params("arbitrary"),
        name="moe_combine",
    )(pos, y, route, res, g.reshape(1, d), b.reshape(1, d))


def _moe_layer(streams, layer, raw, w, ln_g, ln_b, alpha):
    n_exp = raw["moe_w_router"].shape[-1]
    routes = [_router(s["x"], raw["moe_w_router"][layer], raw["moe_b_router"][layer]) for s in streams]
    route_all = jnp.concatenate(routes, axis=0)
    x_all = jnp.concatenate([s["x"] for s in streams], axis=0)
    bm = MOE_ROW_TILE
    meta = _route_meta(route_all[:, 2:4].astype(jnp.int32), n_exp, bm)
    xs = _moe_gather(meta["row_token"], x_all)
    h = _swiglu_ws(xs, raw["moe_w_gu"], layer, bm, meta)
    y = _moe_down(meta, h, raw["moe_w_down"], layer, bm)
    off = 0
    for s, route in zip(streams, routes):
        pos = meta["pos"][2 * off:2 * (off + s["m"])]
        s["x"], s["xb"] = _moe_combine(pos, y, route, s["x"], ln_g, ln_b, alpha)
        off += s["m"]


def _ple_body(xb_ref, x_ref, p_ref, wp_ref, wg_ref, o_ref, ob_ref):
    gate = _sigmoid(_dot(xb_ref[...], wg_ref[...]))
    y = x_ref[...] + _dot(p_ref[...], wp_ref[...]) * gate
    o_ref[...] = y
    ob_ref[...] = y.astype(ob_ref.dtype)


def _ple(x, xb, pb, w_proj, w_gate):
    m, d = x.shape
    dp = pb.shape[-1]
    bm = _tile(m, 1024, SUBLANES * 2)
    bn = _tile(d, 512, LANES)
    return pl.pallas_call(
        _ple_body,
        grid=(m // bm, d // bn),
        in_specs=[
            pl.BlockSpec((bm, d), lambda i, j: (i, 0)),
            pl.BlockSpec((bm, bn), lambda i, j: (i, j)),
            pl.BlockSpec((bm, dp), lambda i, j: (i, 0)),
            pl.BlockSpec((dp, bn), lambda i, j: (0, j)),
            pl.BlockSpec((d, bn), lambda i, j: (0, j)),
        ],
        out_specs=[
            pl.BlockSpec((bm, bn), lambda i, j: (i, j)),
            pl.BlockSpec((bm, bn), lambda i, j: (i, j)),
        ],
        out_shape=[jax.ShapeDtypeStruct((m, d), F32), jax.ShapeDtypeStruct((m, d), BF16)],
        compiler_params=_params("parallel", "arbitrary"),
        name="ple",
    )(xb, x, pb, w_proj, w_gate)


def _heads_body(x_ref, w_ref, cos_ref, sin_ref, o_ref, or_ref, *, heads_per_blk, rope_blocks, two_out):
    acc = _dot(x_ref[...], w_ref[...])
    cos = cos_ref[...]
    sin = sin_ref[...]

    def roped():
        return [_rope_head(acc[:, h * LANES:(h + 1) * LANES], cos, sin) for h in range(heads_per_blk)]

    if two_out:
        o_ref[...] = acc.astype(o_ref.dtype)
        for h, y in enumerate(roped()):
            or_ref[:, h * LANES:(h + 1) * LANES] = y.astype(or_ref.dtype)
    else:
        j = pl.program_id(1)
        is_rope = functools.reduce(jnp.logical_or, [j == rb for rb in rope_blocks])

        @pl.when(is_rope)
        def _():
            for h, y in enumerate(roped()):
                o_ref[:, h * LANES:(h + 1) * LANES] = y
                or_ref[:, h * LANES:(h + 1) * LANES] = y.astype(or_ref.dtype)

        @pl.when(jnp.logical_not(is_rope))
        def _():
            o_ref[...] = acc
            or_ref[...] = acc.astype(or_ref.dtype)


def _heads_proj(xb, w, cos, sin, *, two_out, rope_blocks, out_dtypes):
    m, d = xb.shape
    n = w.shape[-1]
    bn = 4 * LANES
    bm = _tile(m, 1024, SUBLANES * 2)
    return pl.pallas_call(
        functools.partial(_heads_body, heads_per_blk=bn // LANES, rope_blocks=rope_blocks, two_out=two_out),
        grid=(m // bm, n // bn),
        in_specs=[
            pl.BlockSpec((bm, d), lambda i, j: (i, 0)),
            pl.BlockSpec((d, bn), lambda i, j: (0, j)),
            pl.BlockSpec((bm, LANES), lambda i, j: (i, 0)),
            pl.BlockSpec((bm, LANES), lambda i, j: (i, 0)),
        ],
        out_specs=[
            pl.BlockSpec((bm, bn), lambda i, j: (i, j)),
            pl.BlockSpec((bm, bn), lambda i, j: (i, j)),
        ],
        out_shape=[jax.ShapeDtypeStruct((m, n), out_dtypes[0]), jax.ShapeDtypeStruct((m, n), out_dtypes[1])],
        compiler_params=_params("parallel", "arbitrary"),
        name="heads_proj",
    )(xb, w, cos, sin)


def _gates_body(x_ref, w_ref, o_ref):
    o_ref[...] = _sigmoid(_dot(x_ref[...], w_ref[...]))


def _gates_proj(xb, w):
    m, d = xb.shape
    n = w.shape[-1]
    bm = _tile(m, 1024, SUBLANES * 2)
    return pl.pallas_call(
        _gates_body,
        grid=(m // bm,),
        in_specs=[pl.BlockSpec((bm, d), lambda i: (i, 0)), pl.BlockSpec((d, n), lambda i: (0, 0))],
        out_specs=pl.BlockSpec((bm, n), lambda i: (i, 0)),
        out_shape=jax.ShapeDtypeStruct((m, n), F32),
        compiler_params=_params("parallel"),
        name="gates_proj",
    )(xb, w)


def _gather_body(pt_ref, x_ref, oc_ref, os_ref):
    n_parts, n_kv = x_ref.shape[1], x_ref.shape[2]
    for part in range(n_parts):
        for g in range(n_kv):
            col = (part % (n_parts // 2)) * n_kv + g
            rows = x_ref[:, part, g, :]
            if part < n_parts // 2:
                oc_ref[:, col * LANES:(col + 1) * LANES] = rows
            else:
                os_ref[:, col * LANES:(col + 1) * LANES] = rows.astype(os_ref.dtype)


def _gather_pages(cache_kv, page_table):
    n_pool, page, n_parts, n_kv, hd = cache_kv.shape
    nb, n_pages = page_table.shape
    half = n_parts * n_kv * hd // 2
    grid_spec = pltpu.PrefetchScalarGridSpec(
        num_scalar_prefetch=1,
        grid=(nb, n_pages),
        in_specs=[pl.BlockSpec((None, page, n_parts, n_kv, hd), lambda b, p, pt: (pt[b, p], 0, 0, 0, 0))],
        out_specs=[
            pl.BlockSpec((None, page, half), lambda b, p, pt: (b, p, 0)),
            pl.BlockSpec((None, page, half), lambda b, p, pt: (b, p, 0)),
        ],
    )
    return pl.pallas_call(
        _gather_body,
        grid_spec=grid_spec,
        out_shape=[
            jax.ShapeDtypeStruct((nb, n_pages * page, half), F32),
            jax.ShapeDtypeStruct((nb, n_pages * page, half), BF16),
        ],
        compiler_params=_params("parallel", "arbitrary"),
        name="gather_pages",
    )(page_table, cache_kv)


def _compress_body(x_ref, pe_ref, w_ref, o_ref, *, nc):
    half = nc // 2
    acc = jnp.zeros((nc, LANES), F32)
    for l in range(CMP_BLOCK):
        xe = x_ref[pl.ds(l, half, stride=2 * CMP_BLOCK), :]
        xo = x_ref[pl.ds(l + CMP_BLOCK, half, stride=2 * CMP_BLOCK), :]
        xl = jnp.concatenate([xe, xo], axis=0) + pe_ref[l:l + 1, :]
        acc = acc + _dot(xl.astype(BF16), w_ref[l])
    o_ref[...] = acc.astype(o_ref.dtype)


def _compress(x3, pe, w, n_kv, nc):
    nb = x3.shape[0]
    tc = nc * CMP_BLOCK
    return pl.pallas_call(
        functools.partial(_compress_body, nc=nc),
        grid=(nb, 2, n_kv),
        in_specs=[
            pl.BlockSpec((None, tc, LANES), lambda b, c, g: (b, 0, c * n_kv + g)),
            pl.BlockSpec((None, CMP_BLOCK, LANES), lambda b, c, g: (c, 0, 0)),
            pl.BlockSpec((None, CMP_BLOCK, LANES, LANES), lambda b, c, g: (c, 0, 0, 0)),
        ],
        out_specs=pl.BlockSpec((None, None, None, nc, LANES), lambda b, c, g: (c, b, g, 0, 0)),
        out_shape=jax.ShapeDtypeStruct((2, nb, n_kv, nc, LANES), BF16),
        compiler_params=_params("parallel", "parallel", "arbitrary"),
        name="compress",
    )(x3, pe, w)


def _cmp_block_of_lane(nc, shape):
    lane = lax.broadcasted_iota(jnp.int32, shape, len(shape) - 1)
    half = nc // 2
    return jnp.where(lane < half, 2 * lane, 2 * (lane - half) + 1)


def _cmp_branch(q, kc, vc, trow, nc, scale):
    s = _dot_nt(q, kc) * scale
    n = _cmp_block_of_lane(nc, s.shape)
    ok = ((n + 1) * CMP_BLOCK - 1) <= trow
    sm = jnp.where(ok, s, NEG)
    e = jnp.exp(sm - jnp.max(sm, axis=-1, keepdims=True))
    p = jnp.where(ok, e / jnp.sum(e, axis=-1, keepdims=True), 0.0)
    return p, _dot(p.astype(BF16), vc)


def _online_update(carry, s, ok, v, scale):
    m, l, acc = carry
    c = scale * math.log2(math.e)
    sm = s if ok is None else jnp.where(ok, s, NEG)
    m_new = jnp.maximum(m, jnp.max(sm, axis=-1, keepdims=True))
    a = jnp.exp2((m - m_new) * c)
    p = jnp.exp2((sm - m_new) * c)
    return m_new, a * l + jnp.sum(p, axis=-1, keepdims=True), a * acc + _dot(p.astype(BF16), v)


def _online_init(rows):
    return (jnp.full((rows, 1), NEG, F32), jnp.zeros((rows, 1), F32), jnp.zeros((rows, LANES), F32))


def _block_bias(unsel_rows, k0, tk):
    nbk = unsel_rows.shape[-1]
    blk = lax.broadcasted_iota(jnp.int32, (nbk, tk), 0)
    kpos = k0 + lax.broadcasted_iota(jnp.int32, (nbk, tk), 1)
    expand = jnp.where(blk == kpos // SEL_BLOCK, -NEG, 0.0).astype(BF16)
    return _dot(unsel_rows, expand)


def _attn_prompt_body(q_ref, qr_ref, gt_ref, kc_ref, vc_ref, ks_ref, vs_ref, kw_ref, vw_ref, o_ref,
                      *, n_rep, tq, seq, nc, nsb, scale):
    t0 = pl.program_id(2) * tq
    rows = n_rep * tq
    q = jnp.concatenate([q_ref[:, r * LANES:(r + 1) * LANES] for r in range(n_rep)], axis=0)
    qr = jnp.concatenate([qr_ref[:, r * LANES:(r + 1) * LANES] for r in range(n_rep)], axis=0)
    trow = t0 + lax.rem(lax.broadcasted_iota(jnp.int32, (rows, 1), 0), tq)

    p_cmp, o_cmp = _cmp_branch(q, kc_ref[...], vc_ref[...], trow, nc, scale)
    imp = p_cmp[0:tq]
    for r in range(1, n_rep):
        imp = imp + p_cmp[r * tq:(r + 1) * tq]
    half = nc // 2
    imp = imp[:, :half] + imp[:, half:nc]
    imp = jnp.concatenate([imp, jnp.zeros((tq, LANES - half), F32)], axis=1)
    imp_t = imp.T[:nsb]

    j = lax.broadcasted_iota(jnp.int32, (nsb, tq), 0)
    tcol = t0 + lax.broadcasted_iota(jnp.int32, (nsb, tq), 1)
    cur = tcol // SEL_BLOCK
    forced = (j == 0) | (j == cur) | (j == cur - 1)
    avail = j * SEL_BLOCK <= tcol
    work = jnp.where(avail, imp_t + FORCE * forced.astype(F32), -1.0)
    sel = jnp.zeros((nsb, tq), F32)
    for _ in range(min(N_SEL, nsb)):
        mx = jnp.max(work, axis=0, keepdims=True)
        idx = jnp.min(jnp.where(work == mx, j, nsb), axis=0, keepdims=True)
        pick = j == idx
        sel = jnp.where(pick, 1.0, sel)
        work = jnp.where(pick, -jnp.inf, work)
    sel = jnp.where(avail, sel, 0.0)
    unsel = (jnp.concatenate([sel, jnp.zeros((LANES - nsb, tq), F32)], axis=0) - 1.0).T
    unsel = unsel.astype(BF16)

    tk = min(512, seq)
    n_tiles = (t0 + tq + tk - 1) // tk

    def sel_step(kt, carry):
        k0 = pl.multiple_of(kt * tk, tk)
        s = _dot_nt(qr, ks_ref[pl.ds(k0, tk), :])
        tpos = t0 + lax.broadcasted_iota(jnp.int32, (tq, tk), 0)
        kpos = k0 + lax.broadcasted_iota(jnp.int32, (tq, tk), 1)
        bias = _block_bias(unsel, k0, tk) + jnp.where(kpos <= tpos, 0.0, NEG)
        s = (s.reshape(n_rep, tq, tk) + bias[None]).reshape(rows, tk)
        return _online_update(carry, s, None, vs_ref[pl.ds(k0, tk), :], scale)

    _, l_sel, acc_sel = lax.fori_loop(0, n_tiles, sel_step, _online_init(rows))
    o_sel = acc_sel / l_sel

    wl = min(WINDOW + tq, seq)
    w0 = pl.multiple_of(jnp.maximum(t0 + tq - wl, 0), tq)
    s = _dot_nt(qr, kw_ref[pl.ds(w0, wl), :])
    dist = (t0 + lax.broadcasted_iota(jnp.int32, (tq, wl), 0)) - (w0 + lax.broadcasted_iota(jnp.int32, (tq, wl), 1))
    bias = jnp.where((dist >= 0) & (dist < WINDOW), 0.0, NEG)
    s = (s.reshape(n_rep, tq, wl) + bias[None]).reshape(rows, wl)
    _, l_win, acc_win = _online_update(_online_init(rows), s, None, vw_ref[pl.ds(w0, wl), :], scale)
    o_win = acc_win / l_win

    for r in range(n_rep):
        sl = slice(r * tq, (r + 1) * tq)
        o = (gt_ref[:, r:r + 1] * o_cmp[sl]
             + gt_ref[:, n_rep + r:n_rep + r + 1] * o_sel[sl]
             + gt_ref[:, 2 * n_rep + r:2 * n_rep + r + 1] * o_win[sl])
        o_ref[:, r * LANES:(r + 1) * LANES] = o.astype(o_ref.dtype)


def _attn_prompt(q, qr, gates_g, cmp_kv, kvb, nb, seq, n_kv):
    m, d = q.shape
    n_rep = d // (n_kv * LANES)
    nc = seq // CMP_BLOCK
    nsb = -(-seq // SEL_BLOCK)
    tq = _tile(seq, 128, LANES)
    nq = seq // tq
    hw = n_rep * LANES
    kv_spec = lambda part: pl.BlockSpec((seq, LANES), lambda b, g, i: (b, part * n_kv + g))
    cmp_spec = lambda c: pl.BlockSpec((None, None, None, nc, LANES), lambda b, g, i: (c, b, g, 0, 0))
    return pl.pallas_call(
        functools.partial(_attn_prompt_body, n_rep=n_rep, tq=tq, seq=seq, nc=nc, nsb=nsb,
                          scale=float(LANES) ** -0.5),
        grid=(nb, n_kv, nq),
        in_specs=[
            pl.BlockSpec((tq, hw), lambda b, g, i: (b * nq + i, g)),
            pl.BlockSpec((tq, hw), lambda b, g, i: (b * nq + i, g)),
            pl.BlockSpec((None, tq, 3 * n_rep), lambda b, g, i: (g, b * nq + i, 0)),
            cmp_spec(0), cmp_spec(1),
            kv_spec(2), kv_spec(3), kv_spec(4), kv_spec(5),
        ],
        out_specs=pl.BlockSpec((tq, hw), lambda b, g, i: (b * nq + i, g)),
        out_shape=jax.ShapeDtypeStruct((m, d), BF16),
        compiler_params=_params("parallel", "parallel", "arbitrary"),
        name="attn_prompt",
    )(q, qr, gates_g, cmp_kv, cmp_kv, kvb, kvb, kvb, kvb)


def _pad_rows(a, rows):
    return jnp.concatenate([a, jnp.zeros((rows - a.shape[0], a.shape[1]), a.dtype)], axis=0)


def _attn_sample_body(q_ref, qr_ref, gt_ref, kc_ref, vc_ref, ksp_ref, vsp_ref, ksn_ref, vsn_ref,
                      kwp_ref, vwp_ref, kwn_ref, vwn_ref, o_ref,
                      *, n_rep, tn, past, nc, nsb, nbk, scale):
    rows = n_rep * tn
    q = jnp.concatenate([q_ref[:, r * LANES:(r + 1) * LANES] for r in range(n_rep)], axis=0).astype(BF16)
    qr = jnp.concatenate([qr_ref[:, r * LANES:(r + 1) * LANES] for r in range(n_rep)], axis=0).astype(BF16)
    trow = past + lax.rem(lax.broadcasted_iota(jnp.int32, (rows, 1), 0), tn)

    p_cmp, o_cmp = _cmp_branch(q, kc_ref[...], vc_ref[...], trow, nc, scale)
    imp = p_cmp[0:tn]
    for r in range(1, n_rep):
        imp = imp + p_cmp[r * tn:(r + 1) * tn]
    half = nc // 2
    imp = imp[:, :half] + imp[:, half:nc]
    imp = jnp.concatenate([imp, jnp.zeros((tn, nbk - half), F32)], axis=1)

    j = lax.broadcasted_iota(jnp.int32, (tn, nbk), 1)
    tcol = past + lax.broadcasted_iota(jnp.int32, (tn, nbk), 0)
    cur = tcol // SEL_BLOCK
    forced = (j == 0) | (j == cur) | (j == cur - 1)
    avail = (j * SEL_BLOCK <= tcol) & (j < nsb)
    work = jnp.where(avail, imp + FORCE * forced.astype(F32), -1.0)
    work = jnp.where(j < nsb, work, -jnp.inf)
    sel = jnp.zeros((tn, nbk), F32)
    for _ in range(min(N_SEL, nsb)):
        mx = jnp.max(work, axis=1, keepdims=True)
        idx = jnp.min(jnp.where(work == mx, j, nbk), axis=1, keepdims=True)
        pick = j == idx
        sel = jnp.where(pick, 1.0, sel)
        work = jnp.where(pick, -jnp.inf, work)
    sel = jnp.where(avail, sel, 0.0)
    unsel_rows = jnp.concatenate([(sel - 1.0).astype(BF16)] * n_rep, axis=0)

    tk = min(1024, past)

    def sel_step(kt, carry):
        k0 = pl.multiple_of(kt * tk, tk)
        s = _dot_nt(qr, ksp_ref[pl.ds(k0, tk), :]) + _block_bias(unsel_rows, k0, tk)
        kpos = k0 + lax.broadcasted_iota(jnp.int32, (rows, tk), 1)
        return _online_update(carry, s, kpos <= trow, vsp_ref[pl.ds(k0, tk), :], scale)

    carry = lax.fori_loop(0, past // tk, sel_step, _online_init(rows))
    kn = _pad_rows(ksn_ref[...], LANES).astype(BF16)
    vn = _pad_rows(vsn_ref[...], LANES).astype(BF16)
    kpos = past + lax.broadcasted_iota(jnp.int32, (rows, LANES), 1)
    s = _dot_nt(qr, kn) + _block_bias(unsel_rows, past, LANES)
    _, l_sel, acc_sel = _online_update(carry, s, kpos <= trow, vn, scale)
    o_sel = acc_sel / l_sel

    wb = kwp_ref.shape[0]
    dist = trow - (past - wb + lax.broadcasted_iota(jnp.int32, (rows, wb), 1))
    ok = (dist >= 0) & (dist < WINDOW)
    carry = _online_update(_online_init(rows), _dot_nt(qr, kwp_ref[...].astype(BF16)), ok,
                           vwp_ref[...].astype(BF16), scale)
    kn = _pad_rows(kwn_ref[...], LANES).astype(BF16)
    vn = _pad_rows(vwn_ref[...], LANES).astype(BF16)
    dist = trow - kpos
    ok = (dist >= 0) & (dist < WINDOW)
    _, l_win, acc_win = _online_update(carry, _dot_nt(qr, kn), ok, vn, scale)
    o_win = acc_win / l_win

    for r in range(n_rep):
        sl = slice(r * tn, (r + 1) * tn)
        o = (gt_ref[:, r:r + 1] * o_cmp[sl]
             + gt_ref[:, n_rep + r:n_rep + r + 1] * o_sel[sl]
             + gt_ref[:, 2 * n_rep + r:2 * n_rep + r + 1] * o_win[sl])
        o_ref[:, r * LANES:(r + 1) * LANES] = o


def _attn_sample(q, qr, gates_g, cmp_kv, sel_past, kv_new, win_past, nb, tn, n_kv):
    m, d = q.shape
    n_rep = d // (n_kv * LANES)
    past = sel_past.shape[1]
    nc = cmp_kv.shape[3]
    nsb = -(-(past + tn) // SEL_BLOCK)
    nbk = -(-nsb // LANES) * LANES
    wb = win_past.shape[1]
    hw = n_rep * LANES
    new_spec = lambda part: pl.BlockSpec((tn, LANES), lambda b, g: (b, part * n_kv + g))
    cmp_spec = lambda c: pl.BlockSpec((None, None, None, nc, LANES), lambda b, g: (c, b, g, 0, 0))
    return pl.pallas_call(
        functools.partial(_attn_sample_body, n_rep=n_rep, tn=tn, past=past, nc=nc, nsb=nsb, nbk=nbk,
                          scale=float(LANES) ** -0.5),
        grid=(nb, n_kv),
        in_specs=[
            pl.BlockSpec((tn, hw), lambda b, g: (b, g)),
            pl.BlockSpec((tn, hw), lambda b, g: (b, g)),
            pl.BlockSpec((None, tn, 3 * n_rep), lambda b, g: (g, b, 0)),
            cmp_spec(0), cmp_spec(1),
            pl.BlockSpec((None, past, LANES), lambda b, g: (b, 0, g)),
            pl.BlockSpec((None, past, LANES), lambda b, g: (b, 0, n_kv + g)),
            new_spec(2), new_spec(3),
            pl.BlockSpec((None, wb, LANES), lambda b, g: (b, 0, g)),
            pl.BlockSpec((None, wb, LANES), lambda b, g: (b, 0, n_kv + g)),
            new_spec(4), new_spec(5),
        ],
        out_specs=pl.BlockSpec((tn, hw), lambda b, g: (b, g)),
        out_shape=jax.ShapeDtypeStruct((m, d), F32),
        compiler_params=_params("parallel", "arbitrary"),
        name="attn_sample",
    )(q, qr, gates_g, cmp_kv, cmp_kv, sel_past, sel_past, kv_new, kv_new, win_past, win_past, kv_new, kv_new)


def _rope_tables(pos):
    half = LANES // 2
    inv = ROPE_THETA ** (-jnp.arange(half, dtype=F32) / half)
    ang = pos.astype(F32)[:, None] * inv[None, :]
    cos, sin = jnp.cos(ang), jnp.sin(ang)
    return jnp.concatenate([cos, cos], axis=-1), jnp.concatenate([-sin, sin], axis=-1)


def _prep_weights(a_w_in, w_kv, cmp_pe, cmp_w, b_w_qg, ffn_w_down, ple_w_proj, ple_w_gate, n_kv):
    d = a_w_in.shape[1]
    n_heads = d // LANES
    n_rep = n_heads // n_kv
    f = ffn_w_down.shape[1]
    fp = -(-f // 1024) * 1024
    w = {}
    w["a_w_in"] = a_w_in.astype(BF16)
    w["w_kv"] = w_kv.astype(BF16)
    w["cmp_pe"] = jnp.transpose(cmp_pe, (1, 0, 2))
    w["cmp_w"] = jnp.transpose(cmp_w, (1, 0, 2, 3)).astype(BF16)
    w["w_q"] = b_w_qg[:, :, :d].astype(BF16)
    wg = b_w_qg[:, :, d:].reshape(-1, d, n_kv, n_rep, 3)
    wg = jnp.transpose(wg, (0, 1, 2, 4, 3)).reshape(-1, d, 3 * n_heads)
    w["w_gates"] = jnp.pad(wg, ((0, 0), (0, 0), (0, LANES - 3 * n_heads))).astype(BF16)
    w["ffn_w_down"] = jnp.pad(ffn_w_down, ((0, 0), (0, fp - f), (0, 0))).astype(BF16)
    w["ffn_f"] = fp
    w["ple_w_proj"] = ple_w_proj.astype(BF16)
    w["ple_w_gate"] = ple_w_gate.astype(BF16)
    return w


def _new_stream(x3, p4, conv_prev, past_ctx):
    nb, seq, d = x3.shape
    m = nb * seq
    past = 0 if past_ctx is None else past_ctx["past"]
    cos1, sin1 = _rope_tables(past + jnp.arange(seq, dtype=jnp.int32))
    x = x3.reshape(m, d)
    return dict(nb=nb, seq=seq, m=m, x=x, xb=x.astype(BF16), pb=p4.reshape(p4.shape[0], m, -1).astype(BF16),
                conv_prev=conv_prev, past_ctx=past_ctx, cos=jnp.tile(cos1, (nb, 1)), sin=jnp.tile(sin1, (nb, 1)),
                conv_states=[], ctx=None)


def _conv_mixer(s, i, w, raw, ln_g, ln_b, alpha):
    d = s["x"].shape[1]
    prev = jnp.zeros((s["nb"], CONV_W - 1, d), F32) if s["conv_prev"] is None else s["conv_prev"][i]
    v, st = _conv_in(s["xb"], w["a_w_in"][i], raw["a_conv_w"][i], prev, s["seq"])
    s["conv_states"].append(st[:, SUBLANES - (CONV_W - 1):])
    s["x"], s["xb"] = _proj_ln_ws(v, raw["a_w_out"][:, None], i, s["x"], ln_g, ln_b, alpha)


def _nsa_mixer(s, li, w, raw, ln_g, ln_b, alpha, n_kv):
    m, d = s["x"].shape
    n_heads = d // LANES
    n_rep = n_heads // n_kv
    prompt = s["past_ctx"] is None
    qdt = BF16 if prompt else F32
    q, qr = _heads_proj(s["xb"], w["w_q"][li], s["cos"], s["sin"], two_out=True, rope_blocks=(),
                        out_dtypes=(qdt, qdt))
    gates = _gates_proj(s["xb"], w["w_gates"][li])[:, :3 * n_heads]
    gates_g = jnp.transpose(gates.reshape(m, n_kv, 3 * n_rep), (1, 0, 2))
    ctx = s["ctx"]
    if prompt:
        o = _attn_prompt(q, qr, gates_g, ctx["cmp_kv"], ctx["kvb"], s["nb"], s["seq"], n_kv)
    else:
        o = _attn_sample(q, qr, gates_g, ctx["cmp_kv"], s["past_ctx"]["sel_past"], ctx["kv_all"],
                         s["past_ctx"]["win_past"], s["nb"], s["seq"], n_kv).astype(BF16)
    s["x"], s["xb"] = _proj_ln_ws(o, raw["b_w_o"][:, None], li, s["x"], ln_g, ln_b, alpha)


def _shared_context(s, w, n_kv):
    kv_all, kvb = _heads_proj(s["xb"], w["w_kv"], s["cos"], s["sin"], two_out=False, rope_blocks=(2, 4),
                              out_dtypes=(F32, BF16))
    if s["past_ctx"] is None:
        cmp_kv = _compress(kv_all.reshape(s["nb"], s["seq"], -1), w["cmp_pe"], w["cmp_w"], n_kv,
                           s["seq"] // CMP_BLOCK)
    else:
        past = s["past_ctx"]["past"]
        assert past % CMP_BLOCK == 0 and s["seq"] < CMP_BLOCK
        cmp_kv = _compress(s["past_ctx"]["cmp_past"], w["cmp_pe"], w["cmp_w"], n_kv, past // CMP_BLOCK)
    s["ctx"] = dict(kv_all=kv_all, kvb=kvb, cmp_kv=cmp_kv)


def _trunk(streams, w, raw, n_kv):
    depth = raw["ln_g"].shape[0]
    n_a = depth // 2
    alpha = float((2 * depth) ** 0.25)
    for i in range(depth):
        ln_g, ln_b = raw["ln_g"][i], raw["ln_b"][i]
        for s in streams:
            if i < n_a:
                _conv_mixer(s, i, w, raw, ln_g[0], ln_b[0], alpha)
            else:
                _nsa_mixer(s, i - n_a, w, raw, ln_g[0], ln_b[0], alpha, n_kv)
        if i % 2 == 0:
            for s in streams:
                h = _swiglu_ws(s["xb"], raw["ffn_w_gu"][:, None], i // 2, _tile(s["m"], 1024, SUBLANES * 2),
                               f_pad=w["ffn_f"])
                s["x"], s["xb"] = _proj_ln(h, w["ffn_w_down"][i // 2], s["x"], ln_g[1], ln_b[1], alpha)
        else:
            _moe_layer(streams, i // 2, raw, w, ln_g[1], ln_b[1], alpha)
        for s in streams:
            s["x"], s["xb"] = _ple(s["x"], s["xb"], s["pb"][i], w["ple_w_proj"][i], w["ple_w_gate"][i])
            if i == n_a - 1:
                _shared_context(s, w, n_kv)
    outs = []
    for s in streams:
        nb, seq, hd_blk = s["nb"], s["seq"], n_kv * LANES
        kv_all = s["ctx"]["kv_all"]
        rows = kv_all[:, :4 * hd_blk].reshape(nb, seq, 4, n_kv, LANES)
        win = kv_all[:, 4 * hd_blk:].reshape(nb, seq, 2, n_kv, LANES)
        outs.append((s["x"].reshape(nb, seq, -1), rows, win, jnp.stack(s["conv_states"])))
    return outs


def kernel(x_prompt, x_sample, cache_kv, cache_win, state_conv, page_table, p_prompt, p_sample,
           a_w_in, a_conv_w, a_w_out, w_kv, cmp_pe, cmp_w, b_w_qg, b_w_o, ffn_w_gu, ffn_w_down,
           moe_w_router, moe_b_router, moe_w_gu, moe_w_down, ple_w_proj, ple_w_gate, ln_g, ln_b):
    n_kv = cache_kv.shape[3]
    raw = dict(a_conv_w=a_conv_w, a_w_out=a_w_out, b_w_o=b_w_o, ffn_w_gu=ffn_w_gu, moe_w_router=moe_w_router,
               moe_b_router=moe_b_router, moe_w_gu=moe_w_gu, moe_w_down=moe_w_down, ln_g=ln_g, ln_b=ln_b)
    w = _prep_weights(a_w_in, w_kv, cmp_pe, cmp_w, b_w_qg, ffn_w_down, ple_w_proj, ple_w_gate, n_kv)

    n_pool, page = cache_kv.shape[:2]
    cmp_past, sel_past = _gather_pages(cache_kv, page_table)
    nb_s, wb = cache_win.shape[:2]
    past_ctx = dict(past=page_table.shape[1] * page, cmp_past=cmp_past, sel_past=sel_past,
                    win_past=cache_win.reshape(nb_s, wb, -1))

    streams = [_new_stream(x_prompt, p_prompt, None, None),
               _new_stream(x_sample, p_sample, state_conv, past_ctx)]
    (y_p, rows_p, win_p, conv_p), (y_s, rows_s, win_s, conv_s) = _trunk(streams, w, raw, n_kv)
    win_state_p = win_p[:, -min(WINDOW, x_prompt.shape[1]):]
    win_state_s = jnp.concatenate([cache_win.astype(win_s.dtype), win_s], axis=1)[:, -wb:]

    return (y_p, y_s, rows_p, rows_s, win_state_p, win_state_s, conv_p, conv_s)
```

```python
import functools
import math

import jax
import jax.numpy as jnp
from jax import lax
from jax.experimental import pallas as pl
from jax.experimental.pallas import tpu as pltpu

F32 = jnp.float32
BF16 = jnp.bfloat16

CONV_W = 3
CMP_BLOCK = 32
SEL_BLOCK = 64
N_SEL = 16
WINDOW = 512
ROPE_THETA = 10000.0
LN_EPS = 1e-5
NEG = -1e30
FORCE = 1e4

LANES = 128
SUBLANES = 8
VMEM_LIMIT_BYTES = 56 * 1024 * 1024


def _params(*sem):
    return pltpu.CompilerParams(dimension_semantics=sem, vmem_limit_bytes=VMEM_LIMIT_BYTES)


def _tile(n, pref, align):
    if n <= pref:
        return n
    t = (pref // align) * align
    while t >= align:
        if n % t == 0:
            return t
        t -= align
    raise ValueError(f"no tile for {n} (pref {pref}, align {align})")


def _dot(a, b):
    return jnp.dot(a, b, preferred_element_type=F32)


def _dot_nt(a, b):
    return lax.dot_general(a, b, (((1,), (1,)), ((), ())), preferred_element_type=F32)


def _sigmoid(x):
    return 1.0 / (1.0 + jnp.exp(-x))


def _rope_head(a, cos, sin_signed):
    return a * cos + pltpu.roll(a, a.shape[-1] // 2, axis=1) * sin_signed


def _layer_norm(z, g, b):
    mu = jnp.mean(z, axis=-1, keepdims=True)
    zc = z - mu
    var = jnp.mean(zc * zc, axis=-1, keepdims=True)
    return zc * lax.rsqrt(var + LN_EPS) * g + b


WS_STAGE_BYTES = 4 * 1024 * 1024


def _ws_body(te_ref, nu_ref, x_ref, w_hbm, *rest, layer, seg_offsets, bn, ck, nj_valid, epilogue, n_extra, n_out):
    extra = rest[:n_extra]
    outs = rest[n_extra:n_extra + n_out]
    cache, stage, acc_ref, sem = rest[n_extra + n_out:]
    j = pl.program_id(0)
    i = pl.program_id(1)
    e = te_ref[i]
    nseg = len(seg_offsets)
    n_chunks = x_ref.shape[1] // ck
    col_ok = j < nj_valid
    valid = jnp.logical_and(i < nu_ref[0], col_ok)
    changed = jnp.logical_and(jnp.logical_or(i == 0, e != te_ref[jnp.maximum(i - 1, 0)]), col_ok)

    def copies(c, slot):
        return [pltpu.make_async_copy(
            w_hbm.at[layer, e, pl.ds(c * ck, ck), pl.ds(pl.multiple_of(off + j * bn, LANES), bn)],
            stage.at[slot, s], sem.at[slot]) for s, off in enumerate(seg_offsets)]

    @pl.when(changed)
    def _():
        for cp in copies(0, 0):
            cp.start()
        for c in range(n_chunks):
            slot = c % 2
            if c + 1 < n_chunks:
                for cp in copies(c + 1, 1 - slot):
                    cp.start()
            for cp in copies(c, slot):
                cp.wait()
            for s in range(nseg):
                cache[c * ck:(c + 1) * ck, s * bn:(s + 1) * bn] = stage[slot, s].astype(BF16)
            part = _dot(x_ref[:, c * ck:(c + 1) * ck], cache[c * ck:(c + 1) * ck, :])
            if c == 0:
                acc_ref[...] = part
            else:
                acc_ref[...] += part

    @pl.when(jnp.logical_and(valid, jnp.logical_not(changed)))
    def _():
        epilogue(lambda c0, c1: _dot(x_ref[...], cache[:, c0:c1]), acc_ref, extra, outs)

    @pl.when(jnp.logical_and(valid, changed))
    def _():
        epilogue(lambda c0, c1: acc_ref[:, c0:c1], acc_ref, extra, outs)

    @pl.when(jnp.logical_not(valid))
    def _():
        for o in outs:
            o[...] = jnp.zeros_like(o)


def _ws_matmul(x, w4, layer, seg_offsets, bn, nj, nj_valid, bm, epilogue, extra, extra_specs, out_shape, out_specs,
               name, meta=None):
    rows, k = x.shape
    n_tiles = rows // bm
    nseg = len(seg_offsets)
    ck = _tile(k, max(LANES, WS_STAGE_BYTES // (2 * nseg * bn * 4)), LANES)
    if meta is None:
        te = jnp.zeros((n_tiles,), jnp.int32)
        nu = jnp.full((1,), n_tiles, jnp.int32)
    else:
        te, nu = meta["tile_expert"], meta["n_used"]
    grid_spec = pltpu.PrefetchScalarGridSpec(
        num_scalar_prefetch=2,
        grid=(nj, n_tiles),
        in_specs=[pl.BlockSpec((bm, k), lambda j, i, te, nu: (jnp.minimum(i, nu[0] - 1), 0)),
                  pl.BlockSpec(memory_space=pl.ANY)]
        + list(extra_specs),
        out_specs=out_specs,
        scratch_shapes=[pltpu.VMEM((k, nseg * bn), BF16), pltpu.VMEM((2, nseg, ck, bn), F32),
                        pltpu.VMEM((bm, nseg * bn), F32), pltpu.SemaphoreType.DMA((2,))],
    )
    return pl.pallas_call(
        functools.partial(_ws_body, layer=layer, seg_offsets=tuple(seg_offsets), bn=bn, ck=ck, nj_valid=nj_valid,
                          epilogue=epilogue, n_extra=len(extra), n_out=len(out_specs)),
        grid_spec=grid_spec,
        out_shape=out_shape,
        compiler_params=_params("arbitrary", "arbitrary"),
        name=name,
    )(te, nu, x, w4, *extra)


EPI_COLS = 256


def _epi_swiglu(mm, acc_ref, extra, outs):
    bn = outs[0].shape[-1]
    cw = _tile(bn, EPI_COLS, LANES)
    for c in range(bn // cw):
        g = mm(c * cw, (c + 1) * cw)
        u = mm(bn + c * cw, bn + (c + 1) * cw)
        outs[0][:, c * cw:(c + 1) * cw] = (g * _sigmoid(g) * u).astype(outs[0].dtype)


def _epi_copy(mm, acc_ref, extra, outs):
    bn = outs[0].shape[-1]
    cw = _tile(bn, EPI_COLS, LANES)
    for c in range(bn // cw):
        outs[0][:, c * cw:(c + 1) * cw] = mm(c * cw, (c + 1) * cw)


def _epi_ln(mm, acc_ref, extra, outs, *, alpha, br):
    res_ref, g_ref, b_ref = extra
    o_ref, ob_ref = outs
    acc_ref[...] = mm(0, acc_ref.shape[1])

    def rows(r, carry):
        sl = pl.ds(pl.multiple_of(r * br, br), br)
        y = _layer_norm(alpha * res_ref[sl, :] + acc_ref[sl, :], g_ref[...], b_ref[...])
        o_ref[sl, :] = y
        ob_ref[sl, :] = y.astype(ob_ref.dtype)
        return carry

    lax.fori_loop(0, o_ref.shape[0] // br, rows, 0)


def _swiglu_ws(x, w_gu4, layer, bm, meta=None, f_pad=None):
    rows = x.shape[0]
    f = w_gu4.shape[-1] // 2
    f_out = f if f_pad is None else f_pad
    bn = _tile(f, 1024, LANES) if f % 1024 == 0 else _tile(f, 256, LANES)
    return _ws_matmul(
        x, w_gu4, layer, (0, f), bn, f_out // bn, f // bn, bm, _epi_swiglu, (), (),
        [jax.ShapeDtypeStruct((rows, f_out), BF16)],
        [pl.BlockSpec((bm, bn), lambda j, i, te, nu: (i, j))], "swiglu_ws", meta)[0]


def _proj_ln_ws(lhs, w4, layer, res, g, b, alpha):
    m, d = res.shape
    bm = _tile(m, 128, SUBLANES * 2)
    br = _tile(bm, 32, SUBLANES * 2)
    row_spec = pl.BlockSpec((bm, d), lambda j, i, te, nu: (i, 0))
    vec_spec = pl.BlockSpec((1, d), lambda j, i, te, nu: (0, 0))
    return _ws_matmul(
        lhs, w4, layer, (0,), d, 1, 1, bm, functools.partial(_epi_ln, alpha=alpha, br=br),
        (res, g.reshape(1, d), b.reshape(1, d)), (row_spec, vec_spec, vec_spec),
        [jax.ShapeDtypeStruct((m, d), F32), jax.ShapeDtypeStruct((m, d), BF16)],
        [row_spec, row_spec], "proj_ln_ws")


def _conv_in_body(x_ref, wb_ref, wc_ref, wh_ref, cw_ref, prev_ref, v_ref, st_ref, conv_ref, *, seq, nseq):
    x = x_ref[...]
    b = _dot(x, wb_ref[...])
    u = _dot(x, wc_ref[...]) * _dot(x, wh_ref[...])
    w0 = cw_ref[0:1, :]
    w1 = cw_ref[1:2, :]
    w2 = cw_ref[2:3, :]
    for s in range(nseq):
        us = u[s * seq:(s + 1) * seq]
        tt = lax.broadcasted_iota(jnp.int32, us.shape, 0)
        p0 = prev_ref[s, 0:1, :]
        p1 = prev_ref[s, 1:2, :]
        u1 = jnp.where(tt >= 1, pltpu.roll(us, 1, axis=0), p1)
        u2 = jnp.where(tt >= 2, pltpu.roll(us, 2, axis=0), jnp.where(tt == 0, p0, p1))
        conv_ref[s * seq:(s + 1) * seq, :] = u2 * w0 + u1 * w1 + us * w2
        st_ref[s] = us[seq - SUBLANES:seq]
    v_ref[...] = (b * conv_ref[...]).astype(v_ref.dtype)


def _conv_in(xb, w_in, conv_w, prev, seq):
    m, d = xb.shape
    nb_seq = m // seq
    nseq = 1 if seq >= 512 else nb_seq
    bm = nseq * seq
    bn = _tile(d, 256, LANES)
    nb = d // bn
    x_mode = pl.Buffered(1) if bm * d * 2 > (8 << 20) else None
    return pl.pallas_call(
        functools.partial(_conv_in_body, seq=seq, nseq=nseq),
        grid=(m // bm, nb),
        in_specs=[
            pl.BlockSpec((bm, d), lambda i, j: (i, 0), pipeline_mode=x_mode),
            pl.BlockSpec((d, bn), lambda i, j: (0, j)),
            pl.BlockSpec((d, bn), lambda i, j: (0, j + nb)),
            pl.BlockSpec((d, bn), lambda i, j: (0, j + 2 * nb)),
            pl.BlockSpec((CONV_W, bn), lambda i, j: (0, j)),
            pl.BlockSpec((nseq, CONV_W - 1, bn), lambda i, j: (i, 0, j)),
        ],
        out_specs=[
            pl.BlockSpec((bm, bn), lambda i, j: (i, j)),
            pl.BlockSpec((nseq, SUBLANES, bn), lambda i, j: (i, 0, j)),
        ],
        out_shape=[
            jax.ShapeDtypeStruct((m, d), BF16),
            jax.ShapeDtypeStruct((nb_seq, SUBLANES, d), F32),
        ],
        scratch_shapes=[pltpu.VMEM((bm, bn), F32)],
        compiler_params=_params("parallel", "arbitrary"),
        name="conv_in",
    )(xb, w_in, w_in, w_in, conv_w, prev)


def _proj_ln_body(lhs_ref, w_ref, res_ref, g_ref, b_ref, o_ref, ob_ref, *, nk, alpha, bn, br):
    k = pl.program_id(1)
    bm, d = o_ref.shape

    @pl.when(k == 0)
    def _():
        o_ref[...] = jnp.zeros_like(o_ref)

    lhs = lhs_ref[...]
    for c in range(d // bn):
        o_ref[:, c * bn:(c + 1) * bn] += _dot(lhs, w_ref[:, c * bn:(c + 1) * bn])

    @pl.when(k == nk - 1)
    def _():
        def rows(r, carry):
            sl = pl.ds(pl.multiple_of(r * br, br), br)
            y = _layer_norm(alpha * res_ref[sl, :] + o_ref[sl, :], g_ref[...], b_ref[...])
            o_ref[sl, :] = y
            ob_ref[sl, :] = y.astype(ob_ref.dtype)
            return carry

        lax.fori_loop(0, bm // br, rows, 0)


def _proj_ln(lhs, w, res, g, b, alpha):
    m, kdim = lhs.shape
    d = w.shape[-1]
    bm = _tile(m, 512, SUBLANES * 2)
    bk = _tile(kdim, 1024, LANES)
    nk = kdim // bk
    bn = _tile(d, 512, LANES)
    br = _tile(bm, 32, SUBLANES * 2)
    return pl.pallas_call(
        functools.partial(_proj_ln_body, nk=nk, alpha=alpha, bn=bn, br=br),
        grid=(m // bm, nk),
        in_specs=[
            pl.BlockSpec((bm, bk), lambda i, k: (i, k)),
            pl.BlockSpec((bk, d), lambda i, k: (k, 0)),
            pl.BlockSpec((bm, d), lambda i, k: (i, 0), pipeline_mode=pl.Buffered(1)),
            pl.BlockSpec((1, d), lambda i, k: (0, 0)),
            pl.BlockSpec((1, d), lambda i, k: (0, 0)),
        ],
        out_specs=[
            pl.BlockSpec((bm, d), lambda i, k: (i, 0)),
            pl.BlockSpec((bm, d), lambda i, k: (i, 0)),
        ],
        out_shape=[jax.ShapeDtypeStruct((m, d), F32), jax.ShapeDtypeStruct((m, d), BF16)],
        compiler_params=_params("parallel", "arbitrary"),
        name="proj_ln",
    )(lhs, w, res, g.reshape(1, d), b.reshape(1, d))


def _router_body(x_ref, w_ref, b_ref, o_ref, *, n_experts):
    logits = jnp.dot(x_ref[...], w_ref[...], preferred_element_type=F32,
                     precision=lax.Precision.HIGHEST) + b_ref[...]
    lane = lax.broadcasted_iota(jnp.int32, logits.shape, 1)
    lg = jnp.where(lane < n_experts, logits, -jnp.inf)
    m1 = jnp.max(lg, axis=-1, keepdims=True)
    i1 = jnp.min(jnp.where(lg == m1, lane, LANES), axis=-1, keepdims=True)
    lg2 = jnp.where(lane == i1, -jnp.inf, lg)
    m2 = jnp.max(lg2, axis=-1, keepdims=True)
    i2 = jnp.min(jnp.where(lg2 == m2, lane, LANES), axis=-1, keepdims=True)
    e2 = jnp.exp(m2 - m1)
    w1 = 1.0 / (1.0 + e2)
    w2 = e2 / (1.0 + e2)
    o_ref[...] = jnp.where(lane == 0, w1, jnp.where(lane == 1, w2, jnp.where(
        lane == 2, i1.astype(F32), jnp.where(lane == 3, i2.astype(F32), 0.0))))


def _router(x, w_router, b_router):
    m, d = x.shape
    n_experts = w_router.shape[-1]
    w = jnp.pad(w_router, ((0, 0), (0, LANES - n_experts)))
    b = jnp.pad(b_router, (0, LANES - n_experts)).reshape(1, LANES)
    bm = _tile(m, 512, SUBLANES)
    return pl.pallas_call(
        functools.partial(_router_body, n_experts=n_experts),
        grid=(m // bm,),
        in_specs=[
            pl.BlockSpec((bm, d), lambda i: (i, 0)),
            pl.BlockSpec((d, LANES), lambda i: (0, 0)),
            pl.BlockSpec((1, LANES), lambda i: (0, 0)),
        ],
        out_specs=pl.BlockSpec((bm, LANES), lambda i: (i, 0)),
        out_shape=jax.ShapeDtypeStruct((m, LANES), F32),
        compiler_params=_params("parallel"),
        name="router",
    )(x, w, b)


MOE_ROW_TILE = 512


def _route_meta(eid, n_exp, bm):
    m = eid.shape[0]
    a = 2 * m
    n_tiles = -(-a // bm) + n_exp
    e_flat = eid.reshape(a)
    order = jnp.argsort(e_flat, stable=True).astype(jnp.int32)
    e_sorted = e_flat[order]
    counts = jnp.sum(e_flat[:, None] == jnp.arange(n_exp, dtype=jnp.int32)[None, :], axis=0).astype(jnp.int32)
    tiles_e = (counts + bm - 1) // bm
    tile_end = jnp.cumsum(tiles_e).astype(jnp.int32)
    tile_start = tile_end - tiles_e
    first = jnp.cumsum(counts).astype(jnp.int32) - counts
    pos_sorted = tile_start[e_sorted] * bm + (jnp.arange(a, dtype=jnp.int32) - first[e_sorted])
    row_token = jnp.zeros((n_tiles * bm,), jnp.int32).at[pos_sorted].set(order // 2)
    pos = jnp.zeros((a,), jnp.int32).at[order].set(pos_sorted)
    tile_expert = jnp.minimum(jnp.searchsorted(tile_end, jnp.arange(n_tiles, dtype=jnp.int32), side="right"),
                              n_exp - 1).astype(jnp.int32)
    return dict(row_token=row_token, pos=pos, tile_expert=tile_expert, n_used=tile_end[-1:], n_tiles=n_tiles)


def _row_copy(src_hbm, row, dst, dst_row, sem):
    return pltpu.make_async_copy(src_hbm.at[pl.ds(row, 1), :], dst.at[pl.ds(dst_row, 1), :], sem)


def _moe_gather_body(tok_ref, x_hbm, o_ref, buf, sem, *, bm):
    base = pl.program_id(0) * bm

    def issue(r, carry):
        _row_copy(x_hbm, tok_ref[base + r], buf, r, sem).start()
        return carry

    def drain(r, carry):
        _row_copy(x_hbm, 0, buf, r, sem).wait()
        return carry

    lax.fori_loop(0, bm, issue, 0, unroll=8)
    lax.fori_loop(0, bm, drain, 0, unroll=8)
    o_ref[...] = buf[...].astype(o_ref.dtype)


def _moe_gather(row_token, x_all):
    d = x_all.shape[1]
    rows = row_token.shape[0]
    bm = _tile(rows, 512, SUBLANES * 2)
    grid_spec = pltpu.PrefetchScalarGridSpec(
        num_scalar_prefetch=1,
        grid=(rows // bm,),
        in_specs=[pl.BlockSpec(memory_space=pl.ANY)],
        out_specs=pl.BlockSpec((bm, d), lambda i, tok: (i, 0)),
        scratch_shapes=[pltpu.VMEM((bm, d), F32), pltpu.SemaphoreType.DMA],
    )
    return pl.pallas_call(
        functools.partial(_moe_gather_body, bm=bm),
        grid_spec=grid_spec,
        out_shape=jax.ShapeDtypeStruct((rows, d), BF16),
        compiler_params=_params("arbitrary"),
        name="moe_gather",
    )(row_token, x_all)


def _moe_down(meta, h, w_down4, layer, bm):
    rows = h.shape[0]
    d = w_down4.shape[-1]
    bn = _tile(d, 1024, LANES)
    return _ws_matmul(
        h, w_down4, layer, (0,), bn, d // bn, d // bn, bm, _epi_copy, (), (),
        [jax.ShapeDtypeStruct((rows, d), F32)],
        [pl.BlockSpec((bm, bn), lambda j, i, te, nu: (i, j))], "moe_down", meta)[0]


def _moe_combine_body(pos_ref, y_hbm, w_ref, res_ref, g_ref, b_ref, o_ref, ob_ref, buf, sem, *, bm, br, alpha):
    base = pl.program_id(0) * bm

    def issue(r, carry):
        for k in range(2):
            _row_copy(y_hbm, pos_ref[2 * (base + r) + k], buf.at[k], r, sem).start()
        return carry

    def drain(r, carry):
        for k in range(2):
            _row_copy(y_hbm, 0, buf.at[k], r, sem).wait()
        return carry

    lax.fori_loop(0, bm, issue, 0, unroll=8)
    lax.fori_loop(0, bm, drain, 0, unroll=8)

    def rows(r, carry):
        sl = pl.ds(pl.multiple_of(r * br, br), br)
        f = w_ref[sl, 0:1] * buf[0, sl, :] + w_ref[sl, 1:2] * buf[1, sl, :]
        y = _layer_norm(alpha * res_ref[sl, :] + f, g_ref[...], b_ref[...])
        o_ref[sl, :] = y
        ob_ref[sl, :] = y.astype(ob_ref.dtype)
        return carry

    lax.fori_loop(0, bm // br, rows, 0)


def _moe_combine(pos, y, route, res, g, b, alpha):
    m, d = res.shape
    bm = _tile(m, 256, SUBLANES * 2)
    br = _tile(bm, 32, SUBLANES * 2)
    grid_spec = pltpu.PrefetchScalarGridSpec(
        num_scalar_prefetch=1,
        grid=(m // bm,),
        in_specs=[
            pl.BlockSpec(memory_space=pl.ANY),
            pl.BlockSpec((bm, LANES), lambda i, p: (i, 0)),
            pl.BlockSpec((bm, d), lambda i, p: (i, 0)),
            pl.BlockSpec((1, d), lambda i, p: (0, 0)),
            pl.BlockSpec((1, d), lambda i, p: (0, 0)),
        ],
        out_specs=[
            pl.BlockSpec((bm, d), lambda i, p: (i, 0)),
            pl.BlockSpec((bm, d), lambda i, p: (i, 0)),
        ],
        scratch_shapes=[pltpu.VMEM((2, bm, d), F32), pltpu.SemaphoreType.DMA],
    )
    return pl.pallas_call(
        functools.partial(_moe_combine_body, bm=bm, br=br, alpha=alpha),
        grid_spec=grid_spec,
        out_shape=[jax.ShapeDtypeStruct((m, d), F32), jax.ShapeDtypeStruct((m, d), BF16)],
        compiler_params=_params("arbitrary"),
        name="moe_combine",
    )(pos, y, route, res, g.reshape(1, d), b.reshape(1, d))


def _moe_layer(streams, layer, raw, w, ln_g, ln_b, alpha):
    n_exp = raw["moe_w_router"].shape[-1]
    routes = [_router(s["x"], raw["moe_w_router"][layer], raw["moe_b_router"][layer]) for s in streams]
    route_all = jnp.concatenate(routes, axis=0)
    x_all = jnp.concatenate([s["x"] for s in streams], axis=0)
    bm = MOE_ROW_TILE
    meta = _route_meta(route_all[:, 2:4].astype(jnp.int32), n_exp, bm)
    xs = _moe_gather(meta["row_token"], x_all)
    h = _swiglu_ws(xs, raw["moe_w_gu"], layer, bm, meta)
    y = _moe_down(meta, h, raw["moe_w_down"], layer, bm)
    off = 0
    for s, route in zip(streams, routes):
        pos = meta["pos"][2 * off:2 * (off + s["m"])]
        s["x"], s["xb"] = _moe_combine(pos, y, route, s["x"], ln_g, ln_b, alpha)
        off += s["m"]


def _ple_body(xb_ref, x_ref, p_ref, wp_ref, wg_ref, o_ref, ob_ref):
    gate = _sigmoid(_dot(xb_ref[...], wg_ref[...]))
    y = x_ref[...] + _dot(p_ref[...], wp_ref[...]) * gate
    o_ref[...] = y
    ob_ref[...] = y.astype(ob_ref.dtype)


def _ple(x, xb, pb, w_proj, w_gate):
    m, d = x.shape
    dp = pb.shape[-1]
    bm = _tile(m, 1024, SUBLANES * 2)
    bn = _tile(d, 512, LANES)
    return pl.pallas_call(
        _ple_body,
        grid=(m // bm, d // bn),
        in_specs=[
            pl.BlockSpec((bm, d), lambda i, j: (i, 0)),
            pl.BlockSpec((bm, bn), lambda i, j: (i, j)),
            pl.BlockSpec((bm, dp), lambda i, j: (i, 0)),
            pl.BlockSpec((dp, bn), lambda i, j: (0, j)),
            pl.BlockSpec((d, bn), lambda i, j: (0, j)),
        ],
        out_specs=[
            pl.BlockSpec((bm, bn), lambda i, j: (i, j)),
            pl.BlockSpec((bm, bn), lambda i, j: (i, j)),
        ],
        out_shape=[jax.ShapeDtypeStruct((m, d), F32), jax.ShapeDtypeStruct((m, d), BF16)],
        compiler_params=_params("parallel", "arbitrary"),
        name="ple",
    )(xb, x, pb, w_proj, w_gate)


def _heads_body(x_ref, w_ref, cos_ref, sin_ref, o_ref, or_ref, *, heads_per_blk, rope_blocks, two_out):
    acc = _dot(x_ref[...], w_ref[...])
    cos = cos_ref[...]
    sin = sin_ref[...]

    def roped():
        return [_rope_head(acc[:, h * LANES:(h + 1) * LANES], cos, sin) for h in range(heads_per_blk)]

    if two_out:
        o_ref[...] = acc.astype(o_ref.dtype)
        for h, y in enumerate(roped()):
            or_ref[:, h * LANES:(h + 1) * LANES] = y.astype(or_ref.dtype)
    else:
        j = pl.program_id(1)
        is_rope = functools.reduce(jnp.logical_or, [j == rb for rb in rope_blocks])

        @pl.when(is_rope)
        def _():
            for h, y in enumerate(roped()):
                o_ref[:, h * LANES:(h + 1) * LANES] = y
                or_ref[:, h * LANES:(h + 1) * LANES] = y.astype(or_ref.dtype)

        @pl.when(jnp.logical_not(is_rope))
        def _():
            o_ref[...] = acc
            or_ref[...] = acc.astype(or_ref.dtype)


def _heads_proj(xb, w, cos, sin, *, two_out, rope_blocks, out_dtypes):
    m, d = xb.shape
    n = w.shape[-1]
    bn = 4 * LANES
    bm = _tile(m, 1024, SUBLANES * 2)
    return pl.pallas_call(
        functools.partial(_heads_body, heads_per_blk=bn // LANES, rope_blocks=rope_blocks, two_out=two_out),
        grid=(m // bm, n // bn),
        in_specs=[
            pl.BlockSpec((bm, d), lambda i, j: (i, 0)),
            pl.BlockSpec((d, bn), lambda i, j: (0, j)),
            pl.BlockSpec((bm, LANES), lambda i, j: (i, 0)),
            pl.BlockSpec((bm, LANES), lambda i, j: (i, 0)),
        ],
        out_specs=[
            pl.BlockSpec((bm, bn), lambda i, j: (i, j)),
            pl.BlockSpec((bm, bn), lambda i, j: (i, j)),
        ],
        out_shape=[jax.ShapeDtypeStruct((m, n), out_dtypes[0]), jax.ShapeDtypeStruct((m, n), out_dtypes[1])],
        compiler_params=_params("parallel", "arbitrary"),
        name="heads_proj",
    )(xb, w, cos, sin)


def _gates_body(x_ref, w_ref, o_ref):
    o_ref[...] = _sigmoid(_dot(x_ref[...], w_ref[...]))


def _gates_proj(xb, w):
    m, d = xb.shape
    n = w.shape[-1]
    bm = _tile(m, 1024, SUBLANES * 2)
    return pl.pallas_call(
        _gates_body,
        grid=(m // bm,),
        in_specs=[pl.BlockSpec((bm, d), lambda i: (i, 0)), pl.BlockSpec((d, n), lambda i: (0, 0))],
        out_specs=pl.BlockSpec((bm, n), lambda i: (i, 0)),
        out_shape=jax.ShapeDtypeStruct((m, n), F32),
        compiler_params=_params("parallel"),
        name="gates_proj",
    )(xb, w)


def _gather_body(pt_ref, x_ref, oc_ref, os_ref):
    n_parts, n_kv = x_ref.shape[1], x_ref.shape[2]
    for part in range(n_parts):
        for g in range(n_kv):
            col = (part % (n_parts // 2)) * n_kv + g
            rows = x_ref[:, part, g, :]
            if part < n_parts // 2:
                oc_ref[:, col * LANES:(col + 1) * LANES] = rows
            else:
                os_ref[:, col * LANES:(col + 1) * LANES] = rows.astype(os_ref.dtype)


def _gather_pages(cache_kv, page_table):
    n_pool, page, n_parts, n_kv, hd = cache_kv.shape
    nb, n_pages = page_table.shape
    half = n_parts * n_kv * hd // 2
    grid_spec = pltpu.PrefetchScalarGridSpec(
        num_scalar_prefetch=1,
        grid=(nb, n_pages),
        in_specs=[pl.BlockSpec((None, page, n_parts, n_kv, hd), lambda b, p, pt: (pt[b, p], 0, 0, 0, 0))],
        out_specs=[
            pl.BlockSpec((None, page, half), lambda b, p, pt: (b, p, 0)),
            pl.BlockSpec((None, page, half), lambda b, p, pt: (b, p, 0)),
        ],
    )
    return pl.pallas_call(
        _gather_body,
        grid_spec=grid_spec,
        out_shape=[
            jax.ShapeDtypeStruct((nb, n_pages * page, half), F32),
            jax.ShapeDtypeStruct((nb, n_pages * page, half), BF16),
        ],
        compiler_params=_params("parallel", "arbitrary"),
        name="gather_pages",
    )(page_table, cache_kv)


def _compress_body(x_ref, pe_ref, w_ref, o_ref, *, nc):
    half = nc // 2
    acc = jnp.zeros((nc, LANES), F32)
    for l in range(CMP_BLOCK):
        xe = x_ref[pl.ds(l, half, stride=2 * CMP_BLOCK), :]
        xo = x_ref[pl.ds(l + CMP_BLOCK, half, stride=2 * CMP_BLOCK), :]
        xl = jnp.concatenate([xe, xo], axis=0) + pe_ref[l:l + 1, :]
        acc = acc + _dot(xl.astype(BF16), w_ref[l])
    o_ref[...] = acc.astype(o_ref.dtype)


def _compress(x3, pe, w, n_kv, nc):
    nb = x3.shape[0]
    tc = nc * CMP_BLOCK
    return pl.pallas_call(
        functools.partial(_compress_body, nc=nc),
        grid=(nb, 2, n_kv),
        in_specs=[
            pl.BlockSpec((None, tc, LANES), lambda b, c, g: (b, 0, c * n_kv + g)),
            pl.BlockSpec((None, CMP_BLOCK, LANES), lambda b, c, g: (c, 0, 0)),
            pl.BlockSpec((None, CMP_BLOCK, LANES, LANES), lambda b, c, g: (c, 0, 0, 0)),
        ],
        out_specs=pl.BlockSpec((None, None, None, nc, LANES), lambda b, c, g: (c, b, g, 0, 0)),
        out_shape=jax.ShapeDtypeStruct((2, nb, n_kv, nc, LANES), BF16),
        compiler_params=_params("parallel", "parallel", "arbitrary"),
        name="compress",
    )(x3, pe, w)


def _cmp_block_of_lane(nc, shape):
    lane = lax.broadcasted_iota(jnp.int32, shape, len(shape) - 1)
    half = nc // 2
    return jnp.where(lane < half, 2 * lane, 2 * (lane - half) + 1)


def _cmp_branch(q, kc, vc, trow, nc, scale):
    s = _dot_nt(q, kc) * scale
    n = _cmp_block_of_lane(nc, s.shape)
    ok = ((n + 1) * CMP_BLOCK - 1) <= trow
    sm = jnp.where(ok, s, NEG)
    e = jnp.exp(sm - jnp.max(sm, axis=-1, keepdims=True))
    p = jnp.where(ok, e / jnp.sum(e, axis=-1, keepdims=True), 0.0)
    return p, _dot(p.astype(BF16), vc)


def _online_update(carry, s, ok, v, scale):
    m, l, acc = carry
    c = scale * math.log2(math.e)
    sm = s if ok is None else jnp.where(ok, s, NEG)
    m_new = jnp.maximum(m, jnp.max(sm, axis=-1, keepdims=True))
    a = jnp.exp2((m - m_new) * c)
    p = jnp.exp2((sm - m_new) * c)
    return m_new, a * l + jnp.sum(p, axis=-1, keepdims=True), a * acc + _dot(p.astype(BF16), v)


def _online_init(rows):
    return (jnp.full((rows, 1), NEG, F32), jnp.zeros((rows, 1), F32), jnp.zeros((rows, LANES), F32))


def _block_bias(unsel_rows, k0, tk):
    nbk = unsel_rows.shape[-1]
    blk = lax.broadcasted_iota(jnp.int32, (nbk, tk), 0)
    kpos = k0 + lax.broadcasted_iota(jnp.int32, (nbk, tk), 1)
    expand = jnp.where(blk == kpos // SEL_BLOCK, -NEG, 0.0).astype(BF16)
    return _dot(unsel_rows, expand)


def _attn_prompt_body(q_ref, qr_ref, gt_ref, kc_ref, vc_ref, ks_ref, vs_ref, kw_ref, vw_ref, o_ref,
                      *, n_rep, tq, seq, nc, nsb, scale):
    t0 = pl.program_id(2) * tq
    rows = n_rep * tq
    q = jnp.concatenate([q_ref[:, r * LANES:(r + 1) * LANES] for r in range(n_rep)], axis=0)
    qr = jnp.concatenate([qr_ref[:, r * LANES:(r + 1) * LANES] for r in range(n_rep)], axis=0)
    trow = t0 + lax.rem(lax.broadcasted_iota(jnp.int32, (rows, 1), 0), tq)

    p_cmp, o_cmp = _cmp_branch(q, kc_ref[...], vc_ref[...], trow, nc, scale)
    imp = p_cmp[0:tq]
    for r in range(1, n_rep):
        imp = imp + p_cmp[r * tq:(r + 1) * tq]
    half = nc // 2
    imp = imp[:, :half] + imp[:, half:nc]
    imp = jnp.concatenate([imp, jnp.zeros((tq, LANES - half), F32)], axis=1)
    imp_t = imp.T[:nsb]

    j = lax.broadcasted_iota(jnp.int32, (nsb, tq), 0)
    tcol = t0 + lax.broadcasted_iota(jnp.int32, (nsb, tq), 1)
    cur = tcol // SEL_BLOCK
    forced = (j == 0) | (j == cur) | (j == cur - 1)
    avail = j * SEL_BLOCK <= tcol
    work = jnp.where(avail, imp_t + FORCE * forced.astype(F32), -1.0)
    sel = jnp.zeros((nsb, tq), F32)
    for _ in range(min(N_SEL, nsb)):
        mx = jnp.max(work, axis=0, keepdims=True)
        idx = jnp.min(jnp.where(work == mx, j, nsb), axis=0, keepdims=True)
        pick = j == idx
        sel = jnp.where(pick, 1.0, sel)
        work = jnp.where(pick, -jnp.inf, work)
    sel = jnp.where(avail, sel, 0.0)
    unsel = (jnp.concatenate([sel, jnp.zeros((LANES - nsb, tq), F32)], axis=0) - 1.0).T
    unsel = unsel.astype(BF16)

    tk = min(512, seq)
    n_tiles = (t0 + tq + tk - 1) // tk

    def sel_step(kt, carry):
        k0 = pl.multiple_of(kt * tk, tk)
        s = _dot_nt(qr, ks_ref[pl.ds(k0, tk), :])
        tpos = t0 + lax.broadcasted_iota(jnp.int32, (tq, tk), 0)
        kpos = k0 + lax.broadcasted_iota(jnp.int32, (tq, tk), 1)
        bias = _block_bias(unsel, k0, tk) + jnp.where(kpos <= tpos, 0.0, NEG)
        s = (s.reshape(n_rep, tq, tk) + bias[None]).reshape(rows, tk)
        return _online_update(carry, s, None, vs_ref[pl.ds(k0, tk), :], scale)

    _, l_sel, acc_sel = lax.fori_loop(0, n_tiles, sel_step, _online_init(rows))
    o_sel = acc_sel / l_sel

    wl = min(WINDOW + tq, seq)
    w0 = pl.multiple_of(jnp.maximum(t0 + tq - wl, 0), tq)
    s = _dot_nt(qr, kw_ref[pl.ds(w0, wl), :])
    dist = (t0 + lax.broadcasted_iota(jnp.int32, (tq, wl), 0)) - (w0 + lax.broadcasted_iota(jnp.int32, (tq, wl), 1))
    bias = jnp.where((dist >= 0) & (dist < WINDOW), 0.0, NEG)
    s = (s.reshape(n_rep, tq, wl) + bias[None]).reshape(rows, wl)
    _, l_win, acc_win = _online_update(_online_init(rows), s, None, vw_ref[pl.ds(w0, wl), :], scale)
    o_win = acc_win / l_win

    for r in range(n_rep):
        sl = slice(r * tq, (r + 1) * tq)
        o = (gt_ref[:, r:r + 1] * o_cmp[sl]
             + gt_ref[:, n_rep + r:n_rep + r + 1] * o_sel[sl]
             + gt_ref[:, 2 * n_rep + r:2 * n_rep + r + 1] * o_win[sl])
        o_ref[:, r * LANES:(r + 1) * LANES] = o.astype(o_ref.dtype)


def _attn_prompt(q, qr, gates_g, cmp_kv, kvb, nb, seq, n_kv):
    m, d = q.shape
    n_rep = d // (n_kv * LANES)
    nc = seq // CMP_BLOCK
    nsb = -(-seq // SEL_BLOCK)
    tq = _tile(seq, 128, LANES)
    nq = seq // tq
    hw = n_rep * LANES
    kv_spec = lambda part: pl.BlockSpec((seq, LANES), lambda b, g, i: (b, part * n_kv + g))
    cmp_spec = lambda c: pl.BlockSpec((None, None, None, nc, LANES), lambda b, g, i: (c, b, g, 0, 0))
    return pl.pallas_call(
        functools.partial(_attn_prompt_body, n_rep=n_rep, tq=tq, seq=seq, nc=nc, nsb=nsb,
                          scale=float(LANES) ** -0.5),
        grid=(nb, n_kv, nq),
        in_specs=[
            pl.BlockSpec((tq, hw), lambda b, g, i: (b * nq + i, g)),
            pl.BlockSpec((tq, hw), lambda b, g, i: (b * nq + i, g)),
            pl.BlockSpec((None, tq, 3 * n_rep), lambda b, g, i: (g, b * nq + i, 0)),
            cmp_spec(0), cmp_spec(1),
            kv_spec(2), kv_spec(3), kv_spec(4), kv_spec(5),
        ],
        out_specs=pl.BlockSpec((tq, hw), lambda b, g, i: (b * nq + i, g)),
        out_shape=jax.ShapeDtypeStruct((m, d), BF16),
        compiler_params=_params("parallel", "parallel", "arbitrary"),
        name="attn_prompt",
    )(q, qr, gates_g, cmp_kv, cmp_kv, kvb, kvb, kvb, kvb)


def _pad_rows(a, rows):
    return jnp.concatenate([a, jnp.zeros((rows - a.shape[0], a.shape[1]), a.dtype)], axis=0)


def _attn_sample_body(q_ref, qr_ref, gt_ref, kc_ref, vc_ref, ksp_ref, vsp_ref, ksn_ref, vsn_ref,
                      kwp_ref, vwp_ref, kwn_ref, vwn_ref, o_ref,
                      *, n_rep, tn, past, nc, nsb, nbk, scale):
    rows = n_rep * tn
    q = jnp.concatenate([q_ref[:, r * LANES:(r + 1) * LANES] for r in range(n_rep)], axis=0).astype(BF16)
    qr = jnp.concatenate([qr_ref[:, r * LANES:(r + 1) * LANES] for r in range(n_rep)], axis=0).astype(BF16)
    trow = past + lax.rem(lax.broadcasted_iota(jnp.int32, (rows, 1), 0), tn)

    p_cmp, o_cmp = _cmp_branch(q, kc_ref[...], vc_ref[...], trow, nc, scale)
    imp = p_cmp[0:tn]
    for r in range(1, n_rep):
        imp = imp + p_cmp[r * tn:(r + 1) * tn]
    half = nc // 2
    imp = imp[:, :half] + imp[:, half:nc]
    imp = jnp.concatenate([imp, jnp.zeros((tn, nbk - half), F32)], axis=1)

    j = lax.broadcasted_iota(jnp.int32, (tn, nbk), 1)
    tcol = past + lax.broadcasted_iota(jnp.int32, (tn, nbk), 0)
    cur = tcol // SEL_BLOCK
    forced = (j == 0) | (j == cur) | (j == cur - 1)
    avail = (j * SEL_BLOCK <= tcol) & (j < nsb)
    work = jnp.where(avail, imp + FORCE * forced.astype(F32), -1.0)
    work = jnp.where(j < nsb, work, -jnp.inf)
    sel = jnp.zeros((tn, nbk), F32)
    for _ in range(min(N_SEL, nsb)):
        mx = jnp.max(work, axis=1, keepdims=True)
        idx = jnp.min(jnp.where(work == mx, j, nbk), axis=1, keepdims=True)
        pick = j == idx
        sel = jnp.where(pick, 1.0, sel)
        work = jnp.where(pick, -jnp.inf, work)
    sel = jnp.where(avail, sel, 0.0)
    unsel_rows = jnp.concatenate([(sel - 1.0).astype(BF16)] * n_rep, axis=0)

    tk = min(1024, past)

    def sel_step(kt, carry):
        k0 = pl.multiple_of(kt * tk, tk)
        s = _dot_nt(qr, ksp_ref[pl.ds(k0, tk), :]) + _block_bias(unsel_rows, k0, tk)
        kpos = k0 + lax.broadcasted_iota(jnp.int32, (rows, tk), 1)
        return _online_update(carry, s, kpos <= trow, vsp_ref[pl.ds(k0, tk), :], scale)

    carry = lax.fori_loop(0, past // tk, sel_step, _online_init(rows))
    kn = _pad_rows(ksn_ref[...], LANES).astype(BF16)
    vn = _pad_rows(vsn_ref[...], LANES).astype(BF16)
    kpos = past + lax.broadcasted_iota(jnp.int32, (rows, LANES), 1)
    s = _dot_nt(qr, kn) + _block_bias(unsel_rows, past, LANES)
    _, l_sel, acc_sel = _online_update(carry, s, kpos <= trow, vn, scale)
    o_sel = acc_sel / l_sel

    wb = kwp_ref.shape[0]
    dist = trow - (past - wb + lax.broadcasted_iota(jnp.int32, (rows, wb), 1))
    ok = (dist >= 0) & (dist < WINDOW)
    carry = _online_update(_online_init(rows), _dot_nt(qr, kwp_ref[...].astype(BF16)), ok,
                           vwp_ref[...].astype(BF16), scale)
    kn = _pad_rows(kwn_ref[...], LANES).astype(BF16)
    vn = _pad_rows(vwn_ref[...], LANES).astype(BF16)
    dist = trow - kpos
    ok = (dist >= 0) & (dist < WINDOW)
    _, l_win, acc_win = _online_update(carry, _dot_nt(qr, kn), ok, vn, scale)
    o_win = acc_win / l_win

    for r in range(n_rep):
        sl = slice(r * tn, (r + 1) * tn)
        o = (gt_ref[:, r:r + 1] * o_cmp[sl]
             + gt_ref[:, n_rep + r:n_rep + r + 1] * o_sel[sl]
             + gt_ref[:, 2 * n_rep + r:2 * n_rep + r + 1] * o_win[sl])
        o_ref[:, r * LANES:(r + 1) * LANES] = o


def _attn_sample(q, qr, gates_g, cmp_kv, sel_past, kv_new, win_past, nb, tn, n_kv):
    m, d = q.shape
    n_rep = d // (n_kv * LANES)
    past = sel_past.shape[1]
    nc = cmp_kv.shape[3]
    nsb = -(-(past + tn) // SEL_BLOCK)
    nbk = -(-nsb // LANES) * LANES
    wb = win_past.shape[1]
    hw = n_rep * LANES
    new_spec = lambda part: pl.BlockSpec((tn, LANES), lambda b, g: (b, part * n_kv + g))
    cmp_spec = lambda c: pl.BlockSpec((None, None, None, nc, LANES), lambda b, g: (c, b, g, 0, 0))
    return pl.pallas_call(
        functools.partial(_attn_sample_body, n_rep=n_rep, tn=tn, past=past, nc=nc, nsb=nsb, nbk=nbk,
                          scale=float(LANES) ** -0.5),
        grid=(nb, n_kv),
        in_specs=[
            pl.BlockSpec((tn, hw), lambda b, g: (b, g)),
            pl.BlockSpec((tn, hw), lambda b, g: (b, g)),
            pl.BlockSpec((None, tn, 3 * n_rep), lambda b, g: (g, b, 0)),
            cmp_spec(0), cmp_spec(1),
            pl.BlockSpec((None, past, LANES), lambda b, g: (b, 0, g)),
            pl.BlockSpec((None, past, LANES), lambda b, g: (b, 0, n_kv + g)),
            new_spec(2), new_spec(3),
            pl.BlockSpec((None, wb, LANES), lambda b, g: (b, 0, g)),
            pl.BlockSpec((None, wb, LANES), lambda b, g: (b, 0, n_kv + g)),
            new_spec(4), new_spec(5),
        ],
        out_specs=pl.BlockSpec((tn, hw), lambda b, g: (b, g)),
        out_shape=jax.ShapeDtypeStruct((m, d), F32),
        compiler_params=_params("parallel", "arbitrary"),
        name="attn_sample",
    )(q, qr, gates_g, cmp_kv, cmp_kv, sel_past, sel_past, kv_new, kv_new, win_past, win_past, kv_new, kv_new)


def _rope_tables(pos):
    half = LANES // 2
    inv = ROPE_THETA ** (-jnp.arange(half, dtype=F32) / half)
    ang = pos.astype(F32)[:, None] * inv[None, :]
    cos, sin = jnp.cos(ang), jnp.sin(ang)
    return jnp.concatenate([cos, cos], axis=-1), jnp.concatenate([-sin, sin], axis=-1)


def _prep_weights(a_w_in, w_kv, cmp_pe, cmp_w, b_w_qg, ffn_w_down, ple_w_proj, ple_w_gate, n_kv):
    d = a_w_in.shape[1]
    n_heads = d // LANES
    n_rep = n_heads // n_kv
    f = ffn_w_down.shape[1]
    fp = -(-f // 1024) * 1024
    w = {}
    w["a_w_in"] = a_w_in.astype(BF16)
    w["w_kv"] = w_kv.astype(BF16)
    w["cmp_pe"] = jnp.transpose(cmp_pe, (1, 0, 2))
    w["cmp_w"] = jnp.transpose(cmp_w, (1, 0, 2, 3)).astype(BF16)
    w["w_q"] = b_w_qg[:, :, :d].astype(BF16)
    wg = b_w_qg[:, :, d:].reshape(-1, d, n_kv, n_rep, 3)
    wg = jnp.transpose(wg, (0, 1, 2, 4, 3)).reshape(-1, d, 3 * n_heads)
    w["w_gates"] = jnp.pad(wg, ((0, 0), (0, 0), (0, LANES - 3 * n_heads))).astype(BF16)
    w["ffn_w_down"] = jnp.pad(ffn_w_down, ((0, 0), (0, fp - f), (0, 0))).astype(BF16)
    w["ffn_f"] = fp
    w["ple_w_proj"] = ple_w_proj.astype(BF16)
    w["ple_w_gate"] = ple_w_gate.astype(BF16)
    return w


def _new_stream(x3, p4, conv_prev, past_ctx):
    nb, seq, d = x3.shape
    m = nb * seq
    past = 0 if past_ctx is None else past_ctx["past"]
    cos1, sin1 = _rope_tables(past + jnp.arange(seq, dtype=jnp.int32))
    x = x3.reshape(m, d)
    return dict(nb=nb, seq=seq, m=m, x=x, xb=x.astype(BF16), pb=p4.reshape(p4.shape[0], m, -1).astype(BF16),
                conv_prev=conv_prev, past_ctx=past_ctx, cos=jnp.tile(cos1, (nb, 1)), sin=jnp.tile(sin1, (nb, 1)),
                conv_states=[], ctx=None)


def _conv_mixer(s, i, w, raw, ln_g, ln_b, alpha):
    d = s["x"].shape[1]
    prev = jnp.zeros((s["nb"], CONV_W - 1, d), F32) if s["conv_prev"] is None else s["conv_prev"][i]
    v, st = _conv_in(s["xb"], w["a_w_in"][i], raw["a_conv_w"][i], prev, s["seq"])
    s["conv_states"].append(st[:, SUBLANES - (CONV_W - 1):])
    s["x"], s["xb"] = _proj_ln_ws(v, raw["a_w_out"][:, None], i, s["x"], ln_g, ln_b, alpha)


def _nsa_mixer(s, li, w, raw, ln_g, ln_b, alpha, n_kv):
    m, d = s["x"].shape
    n_heads = d // LANES
    n_rep = n_heads // n_kv
    prompt = s["past_ctx"] is None
    qdt = BF16 if prompt else F32
    q, qr = _heads_proj(s["xb"], w["w_q"][li], s["cos"], s["sin"], two_out=True, rope_blocks=(),
                        out_dtypes=(qdt, qdt))
    gates = _gates_proj(s["xb"], w["w_gates"][li])[:, :3 * n_heads]
    gates_g = jnp.transpose(gates.reshape(m, n_kv, 3 * n_rep), (1, 0, 2))
    ctx = s["ctx"]
    if prompt:
        o = _attn_prompt(q, qr, gates_g, ctx["cmp_kv"], ctx["kvb"], s["nb"], s["seq"], n_kv)
    else:
        o = _attn_sample(q, qr, gates_g, ctx["cmp_kv"], s["past_ctx"]["sel_past"], ctx["kv_all"],
                         s["past_ctx"]["win_past"], s["nb"], s["seq"], n_kv).astype(BF16)
    s["x"], s["xb"] = _proj_ln_ws(o, raw["b_w_o"][:, None], li, s["x"], ln_g, ln_b, alpha)


def _shared_context(s, w, n_kv):
    kv_all, kvb = _heads_proj(s["xb"], w["w_kv"], s["cos"], s["sin"], two_out=False, rope_blocks=(2, 4),
                              out_dtypes=(F32, BF16))
    if s["past_ctx"] is None:
        cmp_kv = _compress(kv_all.reshape(s["nb"], s["seq"], -1), w["cmp_pe"], w["cmp_w"], n_kv,
                           s["seq"] // CMP_BLOCK)
    else:
        past = s["past_ctx"]["past"]
        assert past % CMP_BLOCK == 0 and s["seq"] < CMP_BLOCK
        cmp_kv = _compress(s["past_ctx"]["cmp_past"], w["cmp_pe"], w["cmp_w"], n_kv, past // CMP_BLOCK)
    s["ctx"] = dict(kv_all=kv_all, kvb=kvb, cmp_kv=cmp_kv)


def _trunk(streams, w, raw, n_kv):
    depth = raw["ln_g"].shape[0]
    n_a = depth // 2
    alpha = float((2 * depth) ** 0.25)
    for i in range(depth):
        ln_g, ln_b = raw["ln_g"][i], raw["ln_b"][i]
        for s in streams:
            if i < n_a:
                _conv_mixer(s, i, w, raw, ln_g[0], ln_b[0], alpha)
            else:
                _nsa_mixer(s, i - n_a, w, raw, ln_g[0], ln_b[0], alpha, n_kv)
        if i % 2 == 0:
            for s in streams:
                h = _swiglu_ws(s["xb"], raw["ffn_w_gu"][:, None], i // 2, _tile(s["m"], 1024, SUBLANES * 2),
                               f_pad=w["ffn_f"])
                s["x"], s["xb"] = _proj_ln(h, w["ffn_w_down"][i // 2], s["x"], ln_g[1], ln_b[1], alpha)
        else:
            _moe_layer(streams, i // 2, raw, w, ln_g[1], ln_b[1], alpha)
        for s in streams:
            s["x"], s["xb"] = _ple(s["x"], s["xb"], s["pb"][i], w["ple_w_proj"][i], w["ple_w_gate"][i])
            if i == n_a - 1:
                _shared_context(s, w, n_kv)
    outs = []
    for s in streams:
        nb, seq, hd_blk = s["nb"], s["seq"], n_kv * LANES
        kv_all = s["ctx"]["kv_all"]
        rows = kv_all[:, :4 * hd_blk].reshape(nb, seq, 4, n_kv, LANES)
        win = kv_all[:, 4 * hd_blk:].reshape(nb, seq, 2, n_kv, LANES)
        outs.append((s["x"].reshape(nb, seq, -1), rows, win, jnp.stack(s["conv_states"])))
    return outs


def kernel(x_prompt, x_sample, cache_kv, cache_win, state_conv, page_table, p_prompt, p_sample,
           a_w_in, a_conv_w, a_w_out, w_kv, cmp_pe, cmp_w, b_w_qg, b_w_o, ffn_w_gu, ffn_w_down,
           moe_w_router, moe_b_router, moe_w_gu, moe_w_down, ple_w_proj, ple_w_gate, ln_g, ln_b):
    n_kv = cache_kv.shape[3]
    raw = dict(a_conv_w=a_conv_w, a_w_out=a_w_out, b_w_o=b_w_o, ffn_w_gu=ffn_w_gu, moe_w_router=moe_w_router,
               moe_b_router=moe_b_router, moe_w_gu=moe_w_gu, moe_w_down=moe_w_down, ln_g=ln_g, ln_b=ln_b)
    w = _prep_weights(a_w_in, w_kv, cmp_pe, cmp_w, b_w_qg, ffn_w_down, ple_w_proj, ple_w_gate, n_kv)

    n_pool, page = cache_kv.shape[:2]
    cmp_past, sel_past = _gather_pages(cache_kv, page_table)
    nb_s, wb = cache_win.shape[:2]
    past_ctx = dict(past=page_table.shape[1] * page, cmp_past=cmp_past, sel_past=sel_past,
                    win_past=cache_win.reshape(nb_s, wb, -1))

    streams = [_new_stream(x_prompt, p_prompt, None, None),
               _new_stream(x_sample, p_sample, state_conv, past_ctx)]
    (y_p, rows_p, win_p, conv_p), (y_s, rows_s, win_s, conv_s) = _trunk(streams, w, raw, n_kv)
    win_state_p = win_p[:, -min(WINDOW, x_prompt.shape[1]):]
    win_state_s = jnp.concatenate([cache_win.astype(win_s.dtype), win_s], axis=1)[:, -wb:]

    return (y_p, y_s, rows_p, rows_s, win_state_p, win_state_s, conv_p, conv_s)
```
